```python
import math
import jax, jax.numpy as jnp
from jax import lax
import numpy as np

D_MODEL = 1024
BATCH = 2
SEQ = 8192
DEPTH = 4

N_MIXERS = 2
N_S5_LAYERS = (DEPTH + 1) // 2
N_ATTN_LAYERS = DEPTH // 2
N_META = 16
GRID_W = 64
HEAD_DIM = 64
N_Q_HEADS = D_MODEL // HEAD_DIM
N_KV_HEADS = N_Q_HEADS // 4
Q_PER_KV = N_Q_HEADS // N_KV_HEADS
QKV_WIDTH = (N_Q_HEADS + 2 * N_KV_HEADS) * HEAD_DIM
QUERY_BLOCK = 128
ROPE_THETA = 10000.0
ROPE_AXIS_DIM = HEAD_DIM // 2
QK_EPS = 1e-6
S5_GROUP_CH = 16
S5_GROUPS = D_MODEL // S5_GROUP_CH
S5_STATE = 64
S5_DT_MIN = 1e-3
S5_DT_MAX = 1e-1
D_FF = -(-8 * D_MODEL // (3 * 256)) * 256
LN_EPS = 1e-5
DEEPNORM_ALPHA = (2.0 * DEPTH) ** 0.25
DEEPNORM_BETA = (8.0 * DEPTH) ** -0.25

kernel_name = "hybrid_s5_gqa_deepnorm_encoder"


def layer_norm(x, gain, bias):
    xf = x.astype(jnp.float32)
    mean = jnp.mean(xf, axis=-1, keepdims=True)
    var = jnp.mean(jnp.square(xf - mean), axis=-1, keepdims=True)
    y = (xf - mean) * lax.rsqrt(var + LN_EPS) * gain.astype(jnp.float32) + bias.astype(jnp.float32)
    return y.astype(x.dtype)


def swiglu_ffn(h, w_gate, w_up, w_down):
    return (jax.nn.silu(h @ w_gate) * (h @ w_up)) @ w_down


def _s5_combine(left, right):
    ar1, ai1, br1, bi1 = left
    ar2, ai2, br2, bi2 = right
    ar = ar1 * ar2 - ai1 * ai2
    ai = ar1 * ai2 + ai1 * ar2
    br = ar2 * br1 - ai2 * bi1 + br2
    bi = ar2 * bi1 + ai2 * br1 + bi2
    return (ar, ai, br, bi)


def s5_direction(u, lam_re, lam_im, log_dt, b_re, b_im, c_re, c_im, reverse):
    L = u.shape[1]
    lr = lam_re.astype(jnp.float32)
    li = lam_im.astype(jnp.float32)
    dt = jnp.exp(log_dt.astype(jnp.float32))[:, None]
    mag = jnp.exp(lr * dt)
    abr = mag * jnp.cos(li * dt)
    abi = mag * jnp.sin(li * dt)
    nr, ni = abr - 1.0, abi
    den = lr * lr + li * li
    cr = (nr * lr + ni * li) / den
    ci = (ni * lr - nr * li) / den
    br = b_re.astype(jnp.float32)
    bi = b_im.astype(jnp.float32)
    bbr = cr[..., None] * br - ci[..., None] * bi
    bbi = cr[..., None] * bi + ci[..., None] * br
    bu_r = jnp.einsum('blgc,gpc->blgp', u, bbr)
    bu_i = jnp.einsum('blgc,gpc->blgp', u, bbi)
    a_r = jnp.broadcast_to(abr[None, None], (1, L) + abr.shape)
    a_i = jnp.broadcast_to(abi[None, None], (1, L) + abi.shape)
    _, _, sr, si = lax.associative_scan(_s5_combine, (a_r, a_i, bu_r, bu_i),
                                        reverse=reverse, axis=1)
    return (jnp.einsum('blgp,gcp->blgc', sr, c_re.astype(jnp.float32))
            - jnp.einsum('blgp,gcp->blgc', si, c_im.astype(jnp.float32)))


def s5_mixer(h, lam_re, lam_im, log_dt, b_re, b_im, c_re, c_im, d_skip, w_glu, w_out):
    bsz, L, _ = h.shape
    hf = h.astype(jnp.float32)
    u = hf.reshape(bsz, L, S5_GROUPS, S5_GROUP_CH)
    y = (s5_direction(u, lam_re[0], lam_im[0], log_dt[0], b_re[0], b_im[0], c_re[0], c_im[0], False)
         + s5_direction(u, lam_re[1], lam_im[1], log_dt[1], b_re[1], b_im[1], c_re[1], c_im[1], True))
    y = y.reshape(bsz, L, D_MODEL) + d_skip.astype(jnp.float32) * hf
    g = jax.nn.gelu(y, approximate=False).astype(h.dtype)
    z = g * jax.nn.sigmoid(g @ w_glu)
    return z @ w_out


def axial_rope_tables(n_real):
    rows = n_real // GRID_W
    row_ids = jnp.repeat(jnp.arange(rows, dtype=jnp.int32), GRID_W)
    col_ids = jnp.tile(jnp.arange(GRID_W, dtype=jnp.int32), rows)
    pad = jnp.zeros((N_META,), jnp.int32)
    row_ids = jnp.concatenate([pad, row_ids]).astype(jnp.float32)
    col_ids = jnp.concatenate([pad, col_ids]).astype(jnp.float32)
    inv_freq = ROPE_THETA ** (-jnp.arange(0, ROPE_AXIS_DIM, 2, dtype=jnp.float32) / ROPE_AXIS_DIM)
    ang_r = row_ids[:, None] * inv_freq[None, :]
    ang_c = col_ids[:, None] * inv_freq[None, :]
    return jnp.cos(ang_r), jnp.sin(ang_r), jnp.cos(ang_c), jnp.sin(ang_c)


def _rotate_half_block(xh, cos, sin):
    half = ROPE_AXIS_DIM // 2
    x1, x2 = xh[..., :half], xh[..., half:]
    c = cos[None, :, None, :]
    s = sin[None, :, None, :]
    return jnp.concatenate([x1 * c - x2 * s, x2 * c + x1 * s], axis=-1)


def rms_axial_rope(t, gain, cos_r, sin_r, cos_c, sin_c):
    tf = t.astype(jnp.float32)
    tf = tf * lax.rsqrt(jnp.mean(jnp.square(tf), axis=-1, keepdims=True) + QK_EPS) * gain.astype(jnp.float32)
    return jnp.concatenate([_rotate_half_block(tf[..., :ROPE_AXIS_DIM], cos_r, sin_r),
                            _rotate_half_block(tf[..., ROPE_AXIS_DIM:], cos_c, sin_c)], axis=-1)


def _attend(q, k, v):
    s = jnp.einsum('bqkgd,bskd->bkgqs', q, k).astype(jnp.float32) * (HEAD_DIM ** -0.5)
    p = jax.nn.softmax(s, axis=-1).astype(v.dtype)
    return jnp.einsum('bkgqs,bskd->bqkgd', p, v)


def gqa_mixer(h, w_qkv, q_gain, k_gain, w_out, cos_r, sin_r, cos_c, sin_c):
    bsz, L, _ = h.shape
    qkv = h @ w_qkv
    nq, nk = N_Q_HEADS * HEAD_DIM, N_KV_HEADS * HEAD_DIM
    q = qkv[..., :nq].reshape(bsz, L, N_Q_HEADS, HEAD_DIM)
    k = qkv[..., nq:nq + nk].reshape(bsz, L, N_KV_HEADS, HEAD_DIM)
    v = qkv[..., nq + nk:].reshape(bsz, L, N_KV_HEADS, HEAD_DIM)
    q = rms_axial_rope(q, q_gain, cos_r, sin_r, cos_c, sin_c).astype(v.dtype)
    k = rms_axial_rope(k, k_gain, cos_r, sin_r, cos_c, sin_c).astype(v.dtype)
    q = q.reshape(bsz, L, N_KV_HEADS, Q_PER_KV, HEAD_DIM)
    out_meta = _attend(q[:, :N_META], k, v)
    n_real = L - N_META
    n_blk = n_real // QUERY_BLOCK
    qb = q[:, N_META:].reshape(bsz, n_blk, QUERY_BLOCK, N_KV_HEADS, Q_PER_KV, HEAD_DIM).swapaxes(0, 1)
    out_real = lax.map(lambda blk: _attend(blk, k, v), qb)
    out_real = out_real.swapaxes(0, 1).reshape(bsz, n_real, N_KV_HEADS, Q_PER_KV, HEAD_DIM)
    out = jnp.concatenate([out_meta, out_real], axis=1).reshape(bsz, L, D_MODEL)
    return out @ w_out


def setup_inputs(seed: int = 0) -> dict:
    key = jax.random.key(seed)
    ks = jax.random.split(key, 24)
    f32 = jnp.float32
    D, G, P, C = D_MODEL, S5_GROUPS, S5_STATE, S5_GROUP_CH
    nS, nA = N_S5_LAYERS, N_ATTN_LAYERS
    x = jax.random.normal(ks[0], (BATCH, SEQ, D), f32)
    meta_tokens = jax.random.normal(ks[1], (N_META, D), f32)
    s5_lambda_re = -0.5 * jnp.exp(0.02 * jax.random.normal(ks[2], (nS, 2, G, P), f32))
    s5_lambda_im = (math.pi * jnp.arange(P, dtype=f32))[None, None, None, :] \
        + 0.01 * jax.random.normal(ks[3], (nS, 2, G, P), f32)
    s5_log_dt = jax.random.uniform(ks[4], (nS, 2, G), f32,
                                   minval=math.log(S5_DT_MIN), maxval=math.log(S5_DT_MAX))
    b_scale = (2.0 * C) ** -0.5
    s5_b_re = jax.random.normal(ks[5], (nS, 2, G, P, C), f32) * b_scale
    s5_b_im = jax.random.normal(ks[6], (nS, 2, G, P, C), f32) * b_scale
    c_scale = (2.0 * P) ** -0.5
    s5_c_re = jax.random.normal(ks[7], (nS, 2, G, C, P), f32) * c_scale
    s5_c_im = jax.random.normal(ks[8], (nS, 2, G, C, P), f32) * c_scale
    s5_d = jax.random.normal(ks[9], (nS, D), f32)
    s5_w_glu = jax.random.normal(ks[10], (nS, D, D), f32) * D ** -0.5
    s5_w_out = jax.random.normal(ks[11], (nS, D, D), f32) * (D ** -0.5 * DEEPNORM_BETA)
    w_qk = jax.random.normal(ks[12], (nA, D, (N_Q_HEADS + N_KV_HEADS) * HEAD_DIM), f32) * D ** -0.5
    w_v = jax.random.normal(ks[13], (nA, D, N_KV_HEADS * HEAD_DIM), f32) * (D ** -0.5 * DEEPNORM_BETA)
    attn_w_qkv = jnp.concatenate([w_qk, w_v], axis=-1)
    attn_q_gain = 1.0 + 0.02 * jax.random.normal(ks[14], (nA, HEAD_DIM), f32)
    attn_k_gain = 1.0 + 0.02 * jax.random.normal(ks[15], (nA, HEAD_DIM), f32)
    attn_w_out = jax.random.normal(ks[16], (nA, D, D), f32) * (D ** -0.5 * DEEPNORM_BETA)
    ffn_w_gate = jax.random.normal(ks[17], (DEPTH, D, D_FF), f32) * D ** -0.5
    ffn_w_up = jax.random.normal(ks[18], (DEPTH, D, D_FF), f32) * D ** -0.5
    ffn_w_down = jax.random.normal(ks[19], (DEPTH, D_FF, D), f32) * (D_FF ** -0.5 * DEEPNORM_BETA)
    ln_gain = 1.0 + 0.02 * jax.random.normal(ks[20], (DEPTH, 2, D), f32)
    ln_bias = 0.02 * jax.random.normal(ks[21], (DEPTH, 2, D), f32)
    return {"x": x, "meta_tokens": meta_tokens,
            "s5_lambda_re": s5_lambda_re, "s5_lambda_im": s5_lambda_im, "s5_log_dt": s5_log_dt,
            "s5_b_re": s5_b_re, "s5_b_im": s5_b_im, "s5_c_re": s5_c_re, "s5_c_im": s5_c_im,
            "s5_d": s5_d, "s5_w_glu": s5_w_glu, "s5_w_out": s5_w_out,
            "attn_w_qkv": attn_w_qkv, "attn_q_gain": attn_q_gain, "attn_k_gain": attn_k_gain,
            "attn_w_out": attn_w_out,
            "ffn_w_gate": ffn_w_gate, "ffn_w_up": ffn_w_up, "ffn_w_down": ffn_w_down,
            "ln_gain": ln_gain, "ln_bias": ln_bias}


def reference(x, meta_tokens, s5_lambda_re, s5_lambda_im, s5_log_dt, s5_b_re, s5_b_im,
              s5_c_re, s5_c_im, s5_d, s5_w_glu, s5_w_out, attn_w_qkv, attn_q_gain,
              attn_k_gain, attn_w_out, ffn_w_gate, ffn_w_up, ffn_w_down, ln_gain, ln_bias):
    bsz, n_real, d = x.shape
    meta = jnp.broadcast_to(meta_tokens.astype(x.dtype)[None], (bsz, N_META, d))
    h = jnp.concatenate([meta, x], axis=1)
    cos_r, sin_r, cos_c, sin_c = axial_rope_tables(n_real)
    for i in range(DEPTH):
        j = i // N_MIXERS
        if i % N_MIXERS == 0:
            mix = s5_mixer(h, s5_lambda_re[j], s5_lambda_im[j], s5_log_dt[j], s5_b_re[j], s5_b_im[j],
                           s5_c_re[j], s5_c_im[j], s5_d[j], s5_w_glu[j], s5_w_out[j])
        else:
            mix = gqa_mixer(h, attn_w_qkv[j], attn_q_gain[j], attn_k_gain[j], attn_w_out[j],
                            cos_r, sin_r, cos_c, sin_c)
        h = layer_norm(DEEPNORM_ALPHA * h + mix, ln_gain[i, 0], ln_bias[i, 0])
        h = layer_norm(DEEPNORM_ALPHA * h + swiglu_ffn(h, ffn_w_gate[i], ffn_w_up[i], ffn_w_down[i]),
                       ln_gain[i, 1], ln_bias[i, 1])
    return h[:, N_META:]
```

```python
import functools
import math

import jax
import jax.numpy as jnp
from jax import lax
from jax.experimental import pallas as pl
from jax.experimental.pallas import tpu as pltpu

D_MODEL = 1024
BATCH = 2
SEQ = 8192
DEPTH = 4
N_META = 16
GRID_W = 64
HEAD_DIM = 64
N_Q_HEADS = D_MODEL // HEAD_DIM
N_KV_HEADS = N_Q_HEADS // 4
Q_PER_KV = N_Q_HEADS // N_KV_HEADS
Q_WIDTH = N_Q_HEADS * HEAD_DIM
KV_WIDTH = N_KV_HEADS * HEAD_DIM
ROPE_THETA = 10000.0
ROPE_AXIS_DIM = HEAD_DIM // 2
QK_EPS = 1e-6
S5_GROUP_CH = 16
S5_GROUPS = D_MODEL // S5_GROUP_CH
S5_STATE = 64
D_FF = -(-8 * D_MODEL // (3 * 256)) * 256
LN_EPS = 1e-5
DEEPNORM_ALPHA = (2.0 * DEPTH) ** 0.25

LANES = 128
MXU_TILE = 256
CHUNK = 16
N_CHUNKS = SEQ // CHUNK
REAL_ROWS = BATCH * SEQ
CHUNK_ROWS = BATCH * N_CHUNKS
META_ROWS = BATCH * N_META
ROW_TILE = 512
SCAN_ROWS = 16
STATE_W = S5_GROUPS * S5_STATE
GROUPS_PER_TILE = MXU_TILE // S5_GROUP_CH
STATE_TILE = GROUPS_PER_TILE * S5_STATE
N_CH_TILES = D_MODEL // MXU_TILE
SCAN_LANES = 512
CARRY_ROWS = 64
Q_TILE = 256
KEY_TILE = 1024
VMEM_LIMIT = 56 * 1024 * 1024

F32 = jnp.float32
BF16 = jnp.bfloat16


def _const_spec(a):
    return pl.BlockSpec(a.shape, lambda *_, nd=a.ndim: (0,) * nd, pipeline_mode=pl.Buffered(1))


def _params(n_axes):
    return pltpu.CompilerParams(dimension_semantics=("arbitrary",) * n_axes, vmem_limit_bytes=VMEM_LIMIT)


def _dot(a, b):
    return jnp.dot(a, b, preferred_element_type=F32)


def _layer_norm(z, gain, bias):
    mean = jnp.mean(z, axis=-1, keepdims=True)
    zc = z - mean
    var = jnp.mean(zc * zc, axis=-1, keepdims=True)
    return zc * lax.rsqrt(var + LN_EPS) * gain + bias


def _rowwise_call(body, row_inputs, const_inputs, out_dtypes, tile, name):
    rows = row_inputs[0].shape[0]
    row_spec = lambda a: pl.BlockSpec((tile, a.shape[1]), lambda i: (i, 0))
    return pl.pallas_call(
        body,
        grid=(rows // tile,),
        in_specs=[row_spec(a) for a in row_inputs] + [_const_spec(a) for a in const_inputs],
        out_specs=[pl.BlockSpec((tile, D_MODEL), lambda i: (i, 0)) for _ in out_dtypes],
        out_shape=[jax.ShapeDtypeStruct((rows, D_MODEL), dt) for dt in out_dtypes],
        compiler_params=_params(1),
        name=name,
    )(*row_inputs, *const_inputs)


def _ffn_body(h_ref, wg_ref, wu_ref, wd_ref, gain_ref, bias_ref, o_ref):
    h = h_ref[...]
    hb = h.astype(BF16)
    gate = _dot(hb, wg_ref[...])
    up = _dot(hb, wu_ref[...])
    act = (gate * jax.nn.sigmoid(gate) * up).astype(BF16)
    z = DEEPNORM_ALPHA * h + _dot(act, wd_ref[...])
    o_ref[...] = _layer_norm(z, gain_ref[...], bias_ref[...])


def _ffn(h, wg, wu, wd, gain, bias, tile, name):
    return _rowwise_call(_ffn_body, [h], [wg, wu, wd, gain, bias], [F32], tile, name)[0]


def _s5_post_body(h_ref, y_ref, d_ref, wglu_ref, wout_ref, gain_ref, bias_ref, o_ref):
    h = h_ref[...]
    y = y_ref[...] + d_ref[...] * h
    g = 0.5 * y * (1.0 + lax.erf(y * math.sqrt(0.5)))
    gb = g.astype(BF16)
    zz = (g * jax.nn.sigmoid(_dot(gb, wglu_ref[...]))).astype(BF16)
    z = DEEPNORM_ALPHA * h + _dot(zz, wout_ref[...])
    o_ref[...] = _layer_norm(z, gain_ref[...], bias_ref[...])


def _s5_post(h, y, d, wglu, wout, gain, bias, tile, name):
    return _rowwise_call(_s5_post_body, [h, y], [d, wglu, wout, gain, bias], [F32], tile, name)[0]


def _attn_out_body(h_ref, o_ref, wout_ref, gain_ref, bias_ref, out_ref):
    h = h_ref[...]
    z = DEEPNORM_ALPHA * h + _dot(o_ref[0], wout_ref[...])
    out_ref[...] = _layer_norm(z, gain_ref[...], bias_ref[...])


def _attn_out(h, o, wout, gain, bias, tile, name):
    rows = h.shape[0]
    consts = [wout, gain, bias]
    return pl.pallas_call(
        _attn_out_body,
        grid=(rows // tile,),
        in_specs=[pl.BlockSpec((tile, D_MODEL), lambda t: (t, 0)),
                  pl.BlockSpec((1, tile, D_MODEL), lambda t: (t % BATCH, t // BATCH, 0))]
        + [_const_spec(a) for a in consts],
        out_specs=pl.BlockSpec((tile, D_MODEL), lambda t: (t, 0)),
        out_shape=jax.ShapeDtypeStruct((rows, D_MODEL), F32),
        compiler_params=_params(1),
        name=name,
    )(h, o, *consts)


def _rms_rope(t, gain, cos, sin, ones_blk):
    lane = lax.broadcasted_iota(jnp.int32, (t.shape[0], LANES), 1)
    first_half = (lane % ROPE_AXIS_DIM) < (ROPE_AXIS_DIM // 2)
    cols = []
    for c in range(t.shape[1] // LANES):
        tc = t[:, c * LANES:(c + 1) * LANES]
        sq = tc * tc
        hi = sq.astype(BF16)
        lo = (sq - hi.astype(F32)).astype(BF16)
        ms = _dot(hi, ones_blk) + _dot(lo, ones_blk)
        tn = tc * lax.rsqrt(ms + QK_EPS) * gain[:, c * LANES:(c + 1) * LANES]
        half = ROPE_AXIS_DIM // 2
        partner = jnp.where(first_half, pltpu.roll(tn, LANES - half, axis=1), pltpu.roll(tn, half, axis=1))
        cols.append(tn * cos + partner * sin)
    return jnp.concatenate(cols, axis=1)


def _qkv_body(h_ref, w_ref, qg_ref, kg_ref, cos_ref, sin_ref, ones_ref, q_ref, k_ref, v_ref):
    qkv = _dot(h_ref[...].astype(BF16), w_ref[...])
    cos, sin, ones_blk = cos_ref[...], sin_ref[...], ones_ref[...]
    q = _rms_rope(qkv[:, :Q_WIDTH], qg_ref[...], cos, sin, ones_blk)
    k = _rms_rope(qkv[:, Q_WIDTH:Q_WIDTH + KV_WIDTH], kg_ref[...], cos, sin, ones_blk)
    v = qkv[:, Q_WIDTH + KV_WIDTH:]
    q_ref[0] = (q * (HEAD_DIM ** -0.5)).astype(BF16)
    for g in range(N_KV_HEADS):
        k_ref[0, g] = k[:, g * HEAD_DIM:(g + 1) * HEAD_DIM].astype(BF16)
        v_ref[0, g] = v[:, g * HEAD_DIM:(g + 1) * HEAD_DIM].astype(BF16)


def _qkv(h, w, qg, kg, cos, sin, ones_blk, tile, name):
    rows = h.shape[0]
    per_batch = rows // BATCH
    consts_a = [w, qg, kg]
    return pl.pallas_call(
        _qkv_body,
        grid=(rows // tile,),
        in_specs=[pl.BlockSpec((tile, D_MODEL), lambda t: (t, 0))] + [_const_spec(a) for a in consts_a]
        + [pl.BlockSpec((tile, LANES), lambda t: (t, 0)), pl.BlockSpec((tile, LANES), lambda t: (t, 0)),
           _const_spec(ones_blk)],
        out_specs=[pl.BlockSpec((1, tile, Q_WIDTH), lambda t: (t % BATCH, t // BATCH, 0)),
                   pl.BlockSpec((1, N_KV_HEADS, tile, HEAD_DIM), lambda t: (t % BATCH, 0, t // BATCH, 0)),
                   pl.BlockSpec((1, N_KV_HEADS, tile, HEAD_DIM), lambda t: (t % BATCH, 0, t // BATCH, 0))],
        out_shape=[jax.ShapeDtypeStruct((BATCH, per_batch, Q_WIDTH), BF16),
                   jax.ShapeDtypeStruct((BATCH, N_KV_HEADS, per_batch, HEAD_DIM), BF16),
                   jax.ShapeDtypeStruct((BATCH, N_KV_HEADS, per_batch, HEAD_DIM), BF16)],
        compiler_params=_params(1),
        name=name,
    )(h, *consts_a, cos, sin, ones_blk)


def _flash_body(q_ref, k_ref, v_ref, km_ref, vm_ref, o_ref, m_ref, l_ref, acc_ref, *, tq, n_key_tiles):
    qb = q_ref[0]
    q = jnp.concatenate([qb[:, h * HEAD_DIM:(h + 1) * HEAD_DIM] for h in range(Q_PER_KV)], axis=0)
    m_ref[...] = jnp.full(m_ref.shape, -jnp.inf, F32)
    l_ref[...] = jnp.zeros(l_ref.shape, F32)
    acc_ref[...] = jnp.zeros(acc_ref.shape, F32)

    def update(k, v):
        s = lax.dot_general(q, k, (((1,), (1,)), ((), ())), preferred_element_type=F32)
        m_prev = m_ref[...]
        m_new = jnp.maximum(m_prev, jnp.max(s, axis=1, keepdims=True))
        alpha = jnp.exp(m_prev - m_new)
        p = jnp.exp(s - m_new)
        l_ref[...] = alpha * l_ref[...] + jnp.sum(p, axis=1, keepdims=True)
        acc_ref[...] = alpha * acc_ref[...] + _dot(p.astype(BF16), v)
        m_ref[...] = m_new

    update(km_ref[0, 0], vm_ref[0, 0])

    def key_step(j, carry):
        start = pl.multiple_of(j * KEY_TILE, KEY_TILE)
        update(k_ref[0, 0, pl.ds(start, KEY_TILE), :], v_ref[0, 0, pl.ds(start, KEY_TILE), :])
        return carry

    lax.fori_loop(0, n_key_tiles, key_step, 0)
    out = acc_ref[...] / l_ref[...]
    o_ref[0] = jnp.concatenate([out[h * tq:(h + 1) * tq, :] for h in range(Q_PER_KV)], axis=1).astype(BF16)


def _flash(q, k, v, km, vm, tq, name):
    lq = q.shape[1]
    rows = Q_PER_KV * tq
    kv_spec = pl.BlockSpec((1, 1, SEQ, HEAD_DIM), lambda b, g, t: (b, g, 0, 0))
    meta_spec = pl.BlockSpec((1, 1, N_META, HEAD_DIM), lambda b, g, t: (b, g, 0, 0))
    return pl.pallas_call(
        functools.partial(_flash_body, tq=tq, n_key_tiles=SEQ // KEY_TILE),
        grid=(BATCH, N_KV_HEADS, lq // tq),
        in_specs=[pl.BlockSpec((1, tq, Q_PER_KV * HEAD_DIM), lambda b, g, t: (b, t, g)),
                  kv_spec, kv_spec, meta_spec, meta_spec],
        out_specs=pl.BlockSpec((1, tq, Q_PER_KV * HEAD_DIM), lambda b, g, t: (b, t, g)),
        out_shape=jax.ShapeDtypeStruct((BATCH, lq, Q_WIDTH), BF16),
        scratch_shapes=[pltpu.VMEM((rows, 1), F32), pltpu.VMEM((rows, 1), F32),
                        pltpu.VMEM((rows, HEAD_DIM), F32)],
        compiler_params=_params(3),
        name=name,
    )(q, k, v, km, vm)


def _s5_project_in(xb, wb_ref, d, bu_ref):
    for ri in range(2):
        for ct in range(N_CH_TILES):
            bu_ref[ri, :, ct * STATE_TILE:(ct + 1) * STATE_TILE] = _dot(
                xb[:, ct * MXU_TILE:(ct + 1) * MXU_TILE], wb_ref[d, ri, ct])


def _s5_sweep(bu_ref, a_ref, d, lc, sr, si, emit):
    ar = a_ref[d, 0, :, lc:lc + SCAN_LANES]
    ai = a_ref[d, 1, :, lc:lc + SCAN_LANES]
    for step in range(CHUNK):
        i = step if d == 0 else CHUNK - 1 - step
        bur = bu_ref[0, i * SCAN_ROWS:(i + 1) * SCAN_ROWS, lc:lc + SCAN_LANES]
        bui = bu_ref[1, i * SCAN_ROWS:(i + 1) * SCAN_ROWS, lc:lc + SCAN_LANES]
        sr, si = ar * sr - ai * si + bur, ar * si + ai * sr + bui
        emit(i, sr, si)
    return sr, si


def _s5_local_body(x_ref, wb_ref, a_ref, sloc_ref, bu_ref):
    xb = x_ref[...].reshape(CHUNK * SCAN_ROWS, D_MODEL).astype(BF16)
    for d in range(2):
        _s5_project_in(xb, wb_ref, d, bu_ref)
        for lc in range(0, STATE_W, SCAN_LANES):
            zero = jnp.zeros((SCAN_ROWS, SCAN_LANES), F32)
            sr, si = _s5_sweep(bu_ref, a_ref, d, lc, zero, zero, lambda i, sr, si: None)
            sloc_ref[d, :, lc:lc + SCAN_LANES] = sr
            sloc_ref[d, :, STATE_W + lc:STATE_W + lc + SCAN_LANES] = si


def _s5_local(x3, wb, a):
    n_rows = x3.shape[1]
    return pl.pallas_call(
        _s5_local_body,
        grid=(n_rows // SCAN_ROWS,),
        in_specs=[pl.BlockSpec((CHUNK, SCAN_ROWS, D_MODEL), lambda r: (0, r, 0)), _const_spec(wb), _const_spec(a)],
        out_specs=pl.BlockSpec((2, SCAN_ROWS, 2 * STATE_W), lambda r: (0, r, 0)),
        out_shape=jax.ShapeDtypeStruct((2, n_rows, 2 * STATE_W), F32),
        scratch_shapes=[pltpu.VMEM((2, CHUNK * SCAN_ROWS, STATE_W), F32)],
        compiler_params=_params(1),
        name="s5_local",
    )(x3, wb, a)


def _s5_carry_body(sloc_ref, smeta_ref, a16_ref, sin_ref, sin_meta_ref, carry_ref):
    d, b, j = pl.program_id(0), pl.program_id(1), pl.program_id(2)
    n_blocks = pl.num_programs(2)
    fwd = d == 0

    @pl.when((d == 0) & (b == 0) & (j == 0))
    def _():
        sin_meta_ref[...] = jnp.zeros(sin_meta_ref.shape, F32)

    @pl.when(j == 0)
    def _():
        meta_state = smeta_ref[0, pl.ds(b, 1), :]
        carry_ref[0:1, :] = jnp.where(fwd, meta_state, jnp.zeros_like(meta_state))

    ar = a16_ref[pl.ds(d, 1), 0, 0, :][0:1]
    ai = a16_ref[pl.ds(d, 1), 1, 0, :][0:1]

    def row_step(r, _):
        row = jnp.where(fwd, r, CARRY_ROWS - 1 - r)
        c = carry_ref[0:1, :]
        sin_ref[0, pl.ds(row, 1), :] = c
        cr, ci = c[:, :STATE_W], c[:, STATE_W:]
        loc = sloc_ref[0, pl.ds(row, 1), :]
        carry_ref[0:1, :STATE_W] = ar * cr - ai * ci + loc[:, :STATE_W]
        carry_ref[0:1, STATE_W:] = ar * ci + ai * cr + loc[:, STATE_W:]
        return 0

    lax.fori_loop(0, CARRY_ROWS, row_step, 0)

    @pl.when((d == 1) & (j == n_blocks - 1))
    def _():
        sin_meta_ref[1, pl.ds(b, 1), :] = carry_ref[0:1, :]


def _s5_carry(sloc, sloc_meta, a16):
    n_blocks = N_CHUNKS // CARRY_ROWS

    def row_block(d, b, j):
        return (d, b * n_blocks + jnp.where(d == 0, j, n_blocks - 1 - j), 0)

    return pl.pallas_call(
        _s5_carry_body,
        grid=(2, BATCH, n_blocks),
        in_specs=[pl.BlockSpec((1, CARRY_ROWS, 2 * STATE_W), row_block),
                  pl.BlockSpec((1, SCAN_ROWS, 2 * STATE_W), lambda d, b, j: (0, 0, 0)),
                  _const_spec(a16)],
        out_specs=[pl.BlockSpec((1, CARRY_ROWS, 2 * STATE_W), row_block),
                   pl.BlockSpec((2, SCAN_ROWS, 2 * STATE_W), lambda d, b, j: (0, 0, 0))],
        out_shape=[jax.ShapeDtypeStruct(sloc.shape, F32), jax.ShapeDtypeStruct(sloc_meta.shape, F32)],
        scratch_shapes=[pltpu.VMEM((8, 2 * STATE_W), F32)],
        compiler_params=_params(3),
        name="s5_carry",
    )(sloc, sloc_meta, a16)


def _s5_output_body(x_ref, sin_ref, wb_ref, wc_ref, a_ref, y_ref, bu_ref, st_ref):
    xb = x_ref[...].reshape(CHUNK * SCAN_ROWS, D_MODEL).astype(BF16)
    y = None
    for d in range(2):
        _s5_project_in(xb, wb_ref, d, bu_ref)
        for lc in range(0, STATE_W, SCAN_LANES):
            def emit(i, sr, si, lc=lc):
                st_ref[0, i * SCAN_ROWS:(i + 1) * SCAN_ROWS, lc:lc + SCAN_LANES] = sr.astype(BF16)
                st_ref[1, i * SCAN_ROWS:(i + 1) * SCAN_ROWS, lc:lc + SCAN_LANES] = si.astype(BF16)
            _s5_sweep(bu_ref, a_ref, d, lc, sin_ref[d, :, lc:lc + SCAN_LANES],
                      sin_ref[d, :, STATE_W + lc:STATE_W + lc + SCAN_LANES], emit)
        cols = []
        for ct in range(N_CH_TILES):
            sl = slice(ct * STATE_TILE, (ct + 1) * STATE_TILE)
            cols.append(_dot(st_ref[0, :, sl], wc_ref[d, 0, ct]) + _dot(st_ref[1, :, sl], wc_ref[d, 1, ct]))
        yd = jnp.concatenate(cols, axis=1)
        y = yd if y is None else y + yd
    y_ref[...] = y.reshape(CHUNK, SCAN_ROWS, D_MODEL)


def _s5_output(x3, sin, wb, wc, a):
    n_rows = x3.shape[1]
    return pl.pallas_call(
        _s5_output_body,
        grid=(n_rows // SCAN_ROWS,),
        in_specs=[pl.BlockSpec((CHUNK, SCAN_ROWS, D_MODEL), lambda r: (0, r, 0)),
                  pl.BlockSpec((2, SCAN_ROWS, 2 * STATE_W), lambda r: (0, r, 0)),
                  _const_spec(wb), _const_spec(wc), _const_spec(a)],
        out_specs=pl.BlockSpec((CHUNK, SCAN_ROWS, D_MODEL), lambda r: (0, r, 0)),
        out_shape=jax.ShapeDtypeStruct(x3.shape, F32),
        scratch_shapes=[pltpu.VMEM((2, CHUNK * SCAN_ROWS, STATE_W), F32),
                        pltpu.VMEM((2, CHUNK * SCAN_ROWS, STATE_W), BF16)],
        compiler_params=_params(1),
        name="s5_output",
    )(x3, sin, wb, wc, a)


def _s5_weights(lam_re, lam_im, log_dt, b_re, b_im, c_re, c_im):
    dt = jnp.exp(log_dt)[..., None]
    mag = jnp.exp(lam_re * dt)
    abr = mag * jnp.cos(lam_im * dt)
    abi = mag * jnp.sin(lam_im * dt)
    nr, ni = abr - 1.0, abi
    den = lam_re * lam_re + lam_im * lam_im
    cr = (nr * lam_re + ni * lam_im) / den
    ci = (ni * lam_re - nr * lam_im) / den
    bbr = cr[..., None] * b_re - ci[..., None] * b_im
    bbi = cr[..., None] * b_im + ci[..., None] * b_re
    eye = jnp.eye(GROUPS_PER_TILE, dtype=F32)

    def blockdiag_in(m):
        m = m.reshape(2, N_CH_TILES, GROUPS_PER_TILE, S5_STATE, S5_GROUP_CH)
        w = jnp.einsum('dtgpc,gh->dtgchp', m, eye)
        return w.reshape(2, N_CH_TILES, MXU_TILE, STATE_TILE)

    def blockdiag_out(m):
        m = m.reshape(2, N_CH_TILES, GROUPS_PER_TILE, S5_GROUP_CH, S5_STATE)
        w = jnp.einsum('dtgcp,gh->dtgphc', m, eye)
        return w.reshape(2, N_CH_TILES, STATE_TILE, MXU_TILE)

    wb = jnp.stack([blockdiag_in(bbr), blockdiag_in(bbi)], axis=1).astype(BF16)
    wc = jnp.stack([blockdiag_out(c_re), blockdiag_out(-c_im)], axis=1).astype(BF16)
    a = jnp.stack([abr, abi], axis=1).reshape(2, 2, 1, STATE_W)
    pr, pi = abr, abi
    for _ in range(int(math.log2(CHUNK))):
        pr, pi = pr * pr - pi * pi, 2.0 * pr * pi
    a16 = jnp.stack([pr, pi], axis=1).reshape(2, 2, 1, STATE_W)
    return wb, wc, a, a16


def _s5_scan(hr, hm, wb, wc, a, a16):
    x3 = hr.reshape(CHUNK, CHUNK_ROWS, D_MODEL)
    xm3 = jnp.zeros((CHUNK, SCAN_ROWS, D_MODEL), F32).at[:, :BATCH].set(
        hm.reshape(BATCH, N_META, D_MODEL).transpose(1, 0, 2))
    sloc = _s5_local(x3, wb, a)
    sloc_meta = _s5_local(xm3, wb, a)
    sin, sin_meta = _s5_carry(sloc, sloc_meta, a16)
    y3 = _s5_output(x3, sin, wb, wc, a)
    ym3 = _s5_output(xm3, sin_meta, wb, wc, a)
    yr = y3.reshape(REAL_ROWS, D_MODEL)
    ym = ym3[:, :BATCH].transpose(1, 0, 2).reshape(META_ROWS, D_MODEL)
    return yr, ym


def _rope_tables():
    slot = jnp.arange(CHUNK, dtype=jnp.int32)[:, None, None]
    chunk = jnp.arange(N_CHUNKS, dtype=jnp.int32)[None, None, :]
    tok = jnp.broadcast_to(chunk * CHUNK + slot, (CHUNK, BATCH, N_CHUNKS)).reshape(-1)
    row_id = (tok // GRID_W).astype(F32)
    col_id = (tok % GRID_W).astype(F32)
    inv_freq = ROPE_THETA ** (-jnp.arange(0, ROPE_AXIS_DIM, 2, dtype=F32) / ROPE_AXIS_DIM)
    ang_r = row_id[:, None] * inv_freq[None, :]
    ang_c = col_id[:, None] * inv_freq[None, :]
    cos_r, sin_r, cos_c, sin_c = jnp.cos(ang_r), jnp.sin(ang_r), jnp.cos(ang_c), jnp.sin(ang_c)
    cos_head = jnp.concatenate([cos_r, cos_r, cos_c, cos_c], axis=1)
    sin_head = jnp.concatenate([-sin_r, sin_r, -sin_c, sin_c], axis=1)
    reps = LANES // HEAD_DIM
    cos, sin = jnp.tile(cos_head, (1, reps)), jnp.tile(sin_head, (1, reps))
    cos_meta = jnp.ones((META_ROWS, LANES), F32)
    sin_meta = jnp.zeros((META_ROWS, LANES), F32)
    return cos, sin, cos_meta, sin_meta


def kernel(x, meta_tokens, s5_lambda_re, s5_lambda_im, s5_log_dt, s5_b_re, s5_b_im, s5_c_re, s5_c_im, s5_d,
           s5_w_glu, s5_w_out, attn_w_qkv, attn_q_gain, attn_k_gain, attn_w_out, ffn_w_gate, ffn_w_up,
           ffn_w_down, ln_gain, ln_bias):
    hr = x.reshape(BATCH, N_CHUNKS, CHUNK, D_MODEL).transpose(2, 0, 1, 3).reshape(REAL_ROWS, D_MODEL)
    hm = jnp.broadcast_to(meta_tokens[None], (BATCH, N_META, D_MODEL)).reshape(META_ROWS, D_MODEL)
    cos, sin, cos_meta, sin_meta = _rope_tables()
    head_id = jnp.arange(LANES) // HEAD_DIM
    ones_blk = ((head_id[:, None] == head_id[None, :]).astype(F32) / HEAD_DIM).astype(BF16)

    def both(fn, *per_part):
        real = fn(*[p[0] for p in per_part], ROW_TILE, "real")
        meta = fn(*[p[1] for p in per_part], N_META, "meta")
        return real, meta

    for i in range(DEPTH):
        j = i // 2
        gain0, bias0 = ln_gain[i, 0][None], ln_bias[i, 0][None]
        gain1, bias1 = ln_gain[i, 1][None], ln_bias[i, 1][None]
        if i % 2 == 0:
            wb, wc, a, a16 = _s5_weights(s5_lambda_re[j], s5_lambda_im[j], s5_log_dt[j], s5_b_re[j], s5_b_im[j],
                                         s5_c_re[j], s5_c_im[j])
            yr, ym = _s5_scan(hr, hm, wb, wc, a, a16)
            d, wglu, wout = s5_d[j][None], s5_w_glu[j].astype(BF16), s5_w_out[j].astype(BF16)
            hr, hm = both(lambda h, y, tile, tag: _s5_post(h, y, d, wglu, wout, gain0, bias0, tile, f"s5_post_{tag}"),
                          (hr, hm), (yr, ym))
        else:
            wqkv, wout = attn_w_qkv[j].astype(BF16), attn_w_out[j].astype(BF16)
            qg = jnp.tile(attn_q_gain[j], N_Q_HEADS)[None]
            kg = jnp.tile(attn_k_gain[j], N_KV_HEADS)[None]
            (q, k, v), (qm, km, vm) = both(
                lambda h, c, s, tile, tag: _qkv(h, wqkv, qg, kg, c, s, ones_blk, tile, f"qkv_{tag}"),
                (hr, hm), (cos, cos_meta), (sin, sin_meta))
            o = _flash(q, k, v, km, vm, Q_TILE, "flash_real")
            om = _flash(qm, k, v, km, vm, N_META, "flash_meta")
            hr, hm = both(lambda h, oo, tile, tag: _attn_out(h, oo, wout, gain0, bias0, tile, f"attn_out_{tag}"),
                          (hr, hm), (o, om))
        wg, wu, wd = ffn_w_gate[i].astype(BF16), ffn_w_up[i].astype(BF16), ffn_w_down[i].astype(BF16)
        hr, hm = both(lambda h, tile, tag: _ffn(h, wg, wu, wd, gain1, bias1, tile, f"ffn_{tag}"), (hr, hm))
    return hr.reshape(CHUNK, BATCH, N_CHUNKS, D_MODEL).transpose(1, 2, 0, 3).reshape(BATCH, SEQ, D_MODEL)
```

```python
import functools
import math

import jax
import jax.numpy as jnp
from jax import lax
from jax.experimental import pallas as pl
from jax.experimental.pallas import tpu as pltpu

D_MODEL = 1024
BATCH = 2
SEQ = 8192
DEPTH = 4
N_META = 16
GRID_W = 64
HEAD_DIM = 64
N_Q_HEADS = D_MODEL // HEAD_DIM
N_KV_HEADS = N_Q_HEADS // 4
Q_PER_KV = N_Q_HEADS // N_KV_HEADS
Q_WIDTH = N_Q_HEADS * HEAD_DIM
KV_WIDTH = N_KV_HEADS * HEAD_DIM
ROPE_THETA = 10000.0
ROPE_AXIS_DIM = HEAD_DIM // 2
QK_EPS = 1e-6
S5_GROUP_CH = 16
S5_GROUPS = D_MODEL // S5_GROUP_CH
S5_STATE = 64
D_FF = -(-8 * D_MODEL // (3 * 256)) * 256
LN_EPS = 1e-5
DEEPNORM_ALPHA = (2.0 * DEPTH) ** 0.25

LANES = 128
MXU_TILE = 256
CHUNK = 16
N_CHUNKS = SEQ // CHUNK
REAL_ROWS = BATCH * SEQ
CHUNK_ROWS = BATCH * N_CHUNKS
META_ROWS = BATCH * N_META
ROW_TILE = 512
SCAN_ROWS = 16
STATE_W = S5_GROUPS * S5_STATE
GROUPS_PER_TILE = MXU_TILE // S5_GROUP_CH
STATE_TILE = GROUPS_PER_TILE * S5_STATE
N_CH_TILES = D_MODEL // MXU_TILE
SCAN_LANES = 512
CARRY_ROWS = 64
Q_TILE = 256
KEY_TILE = 1024
V_ROWS = 2 * HEAD_DIM
LOG2_E = math.log2(math.e)
VMEM_LIMIT = 56 * 1024 * 1024

F32 = jnp.float32
BF16 = jnp.bfloat16


def _const_spec(a):
    return pl.BlockSpec(a.shape, lambda *_, nd=a.ndim: (0,) * nd, pipeline_mode=pl.Buffered(1))


def _params(n_axes):
    return pltpu.CompilerParams(dimension_semantics=("arbitrary",) * n_axes, vmem_limit_bytes=VMEM_LIMIT)


def _dot(a, b):
    return jnp.dot(a, b, preferred_element_type=F32)


def _layer_norm(z, gain, bias):
    mean = jnp.mean(z, axis=-1, keepdims=True)
    zc = z - mean
    var = jnp.mean(zc * zc, axis=-1, keepdims=True)
    return zc * lax.rsqrt(var + LN_EPS) * gain + bias


def _rowwise_call(body, row_inputs, const_inputs, out_dtypes, tile, name):
    rows = row_inputs[0].shape[0]
    row_spec = lambda a: pl.BlockSpec((tile, a.shape[1]), lambda i: (i, 0))
    return pl.pallas_call(
        body,
        grid=(rows // tile,),
        in_specs=[row_spec(a) for a in row_inputs] + [_const_spec(a) for a in const_inputs],
        out_specs=[pl.BlockSpec((tile, D_MODEL), lambda i: (i, 0)) for _ in out_dtypes],
        out_shape=[jax.ShapeDtypeStruct((rows, D_MODEL), dt) for dt in out_dtypes],
        compiler_params=_params(1),
        name=name,
    )(*row_inputs, *const_inputs)


def _ffn_body(h_ref, wg_ref, wu_ref, wd_ref, gain_ref, bias_ref, o_ref):
    h = h_ref[...]
    hb = h.astype(BF16)
    gate = _dot(hb, wg_ref[...])
    up = _dot(hb, wu_ref[...])
    act = (gate * jax.nn.sigmoid(gate) * up).astype(BF16)
    z = DEEPNORM_ALPHA * h + _dot(act, wd_ref[...])
    o_ref[...] = _layer_norm(z, gain_ref[...], bias_ref[...])


def _ffn(h, wg, wu, wd, gain, bias, tile, name):
    return _rowwise_call(_ffn_body, [h], [wg, wu, wd, gain, bias], [F32], tile, name)[0]


def _s5_post_body(h_ref, y_ref, d_ref, wglu_ref, wout_ref, gain_ref, bias_ref, o_ref):
    h = h_ref[...]
    y = y_ref[...] + d_ref[...] * h
    g = 0.5 * y * (1.0 + lax.erf(y * math.sqrt(0.5)))
    gb = g.astype(BF16)
    zz = (g * jax.nn.sigmoid(_dot(gb, wglu_ref[...]))).astype(BF16)
    z = DEEPNORM_ALPHA * h + _dot(zz, wout_ref[...])
    o_ref[...] = _layer_norm(z, gain_ref[...], bias_ref[...])


def _s5_post(h, y, d, wglu, wout, gain, bias, tile, name):
    return _rowwise_call(_s5_post_body, [h, y], [d, wglu, wout, gain, bias], [F32], tile, name)[0]


def _attn_out_body(h_ref, o_ref, wout_ref, gain_ref, bias_ref, out_ref):
    h = h_ref[...]
    z = DEEPNORM_ALPHA * h + _dot(o_ref[0], wout_ref[...])
    out_ref[...] = _layer_norm(z, gain_ref[...], bias_ref[...])


def _attn_out(h, o, wout, gain, bias, tile, name):
    rows = h.shape[0]
    consts = [wout, gain, bias]
    return pl.pallas_call(
        _attn_out_body,
        grid=(rows // tile,),
        in_specs=[pl.BlockSpec((tile, D_MODEL), lambda t: (t, 0)),
                  pl.BlockSpec((1, tile, D_MODEL), lambda t: (t % BATCH, t // BATCH, 0))]
        + [_const_spec(a) for a in consts],
        out_specs=pl.BlockSpec((tile, D_MODEL), lambda t: (t, 0)),
        out_shape=jax.ShapeDtypeStruct((rows, D_MODEL), F32),
        compiler_params=_params(1),
        name=name,
    )(h, o, *consts)


def _rms_rope(t, gain, cos, sin, ones_blk):
    lane = lax.broadcasted_iota(jnp.int32, (t.shape[0], LANES), 1)
    first_half = (lane % ROPE_AXIS_DIM) < (ROPE_AXIS_DIM // 2)
    cols = []
    for c in range(t.shape[1] // LANES):
        tc = t[:, c * LANES:(c + 1) * LANES]
        sq = tc * tc
        hi = sq.astype(BF16)
        lo = (sq - hi.astype(F32)).astype(BF16)
        ms = _dot(hi, ones_blk) + _dot(lo, ones_blk)
        tn = tc * lax.rsqrt(ms + QK_EPS) * gain[:, c * LANES:(c + 1) * LANES]
        half = ROPE_AXIS_DIM // 2
        partner = jnp.where(first_half, pltpu.roll(tn, LANES - half, axis=1), pltpu.roll(tn, half, axis=1))
        cols.append(tn * cos + partner * sin)
    return jnp.concatenate(cols, axis=1)


def _qkv_body(h_ref, w_ref, qg_ref, kg_ref, cos_ref, sin_ref, ones_ref, q_ref, k_ref, v_ref, *, v_transposed):
    qkv = _dot(h_ref[...].astype(BF16), w_ref[...])
    cos, sin, ones_blk = cos_ref[...], sin_ref[...], ones_ref[...]
    q = _rms_rope(qkv[:, :Q_WIDTH], qg_ref[...], cos, sin, ones_blk)
    k = _rms_rope(qkv[:, Q_WIDTH:Q_WIDTH + KV_WIDTH], kg_ref[...], cos, sin, ones_blk)
    v = qkv[:, Q_WIDTH + KV_WIDTH:]
    q_ref[0] = (q * (HEAD_DIM ** -0.5 * LOG2_E)).astype(BF16)
    if v_transposed:
        vt = v.T
    for g in range(N_KV_HEADS):
        k_ref[0, g] = k[:, g * HEAD_DIM:(g + 1) * HEAD_DIM].astype(BF16)
        if v_transposed:
            v_ref[0, g, 0:HEAD_DIM, :] = vt[g * HEAD_DIM:(g + 1) * HEAD_DIM, :].astype(BF16)
            v_ref[0, g, HEAD_DIM:, :] = jnp.ones((V_ROWS - HEAD_DIM, vt.shape[1]), BF16)
        else:
            v_ref[0, g] = v[:, g * HEAD_DIM:(g + 1) * HEAD_DIM].astype(BF16)


def _qkv(h, w, qg, kg, cos, sin, ones_blk, tile, name, v_transposed):
    rows = h.shape[0]
    per_batch = rows // BATCH
    consts_a = [w, qg, kg]
    if v_transposed:
        v_spec = pl.BlockSpec((1, N_KV_HEADS, V_ROWS, tile), lambda t: (t % BATCH, 0, 0, t // BATCH))
        v_shape = jax.ShapeDtypeStruct((BATCH, N_KV_HEADS, V_ROWS, per_batch), BF16)
    else:
        v_spec = pl.BlockSpec((1, N_KV_HEADS, tile, HEAD_DIM), lambda t: (t % BATCH, 0, t // BATCH, 0))
        v_shape = jax.ShapeDtypeStruct((BATCH, N_KV_HEADS, per_batch, HEAD_DIM), BF16)
    return pl.pallas_call(
        functools.partial(_qkv_body, v_transposed=v_transposed),
        grid=(rows // tile,),
        in_specs=[pl.BlockSpec((tile, D_MODEL), lambda t: (t, 0))] + [_const_spec(a) for a in consts_a]
        + [pl.BlockSpec((tile, LANES), lambda t: (t, 0)), pl.BlockSpec((tile, LANES), lambda t: (t, 0)),
           _const_spec(ones_blk)],
        out_specs=[pl.BlockSpec((1, tile, Q_WIDTH), lambda t: (t % BATCH, t // BATCH, 0)),
                   pl.BlockSpec((1, N_KV_HEADS, tile, HEAD_DIM), lambda t: (t % BATCH, 0, t // BATCH, 0)),
                   v_spec],
        out_shape=[jax.ShapeDtypeStruct((BATCH, per_batch, Q_WIDTH), BF16),
                   jax.ShapeDtypeStruct((BATCH, N_KV_HEADS, per_batch, HEAD_DIM), BF16),
                   v_shape],
        compiler_params=_params(1),
        name=name,
    )(h, *consts_a, cos, sin, ones_blk)


def _flash_body(q_ref, k_ref, vt_ref, km_ref, vtm_ref, o_ref, qs_ref, m_ref, acc_ref, sa_ref, sb_ref, *,
                tq, n_key_tiles):
    qb = q_ref[0]
    for h in range(Q_PER_KV):
        qs_ref[h * tq:(h + 1) * tq, :] = qb[:, h * HEAD_DIM:(h + 1) * HEAD_DIM]
    m_ref[...] = jnp.full(m_ref.shape, -jnp.inf, F32)
    acc_ref[...] = jnp.zeros(acc_ref.shape, F32)

    def scores(k):
        return lax.dot_general(k, qs_ref[...], (((1,), (1,)), ((), ())), preferred_element_type=F32)

    def absorb(st, vt):
        m_prev = m_ref[...]
        m_new = jnp.maximum(m_prev, jnp.max(st, axis=0, keepdims=True))
        alpha = jnp.exp2(m_prev - m_new)
        pt = jnp.exp2(st - m_new).astype(BF16)
        acc_ref[...] = alpha * acc_ref[...] + _dot(vt, pt)
        m_ref[...] = m_new

    def k_tile(j):
        return k_ref[0, 0, pl.ds(pl.multiple_of(j * KEY_TILE, KEY_TILE), KEY_TILE), :]

    def v_tile(j):
        return vt_ref[0, 0, :, pl.ds(pl.multiple_of(j * KEY_TILE, KEY_TILE), KEY_TILE)]

    sa_ref[...] = scores(k_tile(0))
    absorb(scores(km_ref[0, 0]), vtm_ref[0, 0])

    def pair(jj, last):
        sb_ref[...] = scores(k_tile(2 * jj + 1))
        absorb(sa_ref[...], v_tile(2 * jj))
        if not last:
            sa_ref[...] = scores(k_tile(2 * jj + 2))
        absorb(sb_ref[...], v_tile(2 * jj + 1))

    def pair_step(jj, carry):
        pair(jj, False)
        return carry

    n_pairs = n_key_tiles // 2
    lax.fori_loop(0, n_pairs - 1, pair_step, 0)
    pair(n_pairs - 1, True)
    acc = acc_ref[...]
    out = (acc[:HEAD_DIM, :] / acc[HEAD_DIM:HEAD_DIM + 1, :]).T
    o_ref[0] = jnp.concatenate([out[h * tq:(h + 1) * tq, :] for h in range(Q_PER_KV)], axis=1).astype(BF16)


def _flash(q, k, vt, km, vtm, tq, name):
    lq = q.shape[1]
    rows = Q_PER_KV * tq
    head = lambda b, g, t: (b, g, 0, 0)
    return pl.pallas_call(
        functools.partial(_flash_body, tq=tq, n_key_tiles=SEQ // KEY_TILE),
        grid=(BATCH, N_KV_HEADS, lq // tq),
        in_specs=[pl.BlockSpec((1, tq, Q_PER_KV * HEAD_DIM), lambda b, g, t: (b, t, g)),
                  pl.BlockSpec((1, 1, SEQ, HEAD_DIM), head), pl.BlockSpec((1, 1, V_ROWS, SEQ), head),
                  pl.BlockSpec((1, 1, N_META, HEAD_DIM), head), pl.BlockSpec((1, 1, V_ROWS, N_META), head)],
        out_specs=pl.BlockSpec((1, tq, Q_PER_KV * HEAD_DIM), lambda b, g, t: (b, t, g)),
        out_shape=jax.ShapeDtypeStruct((BATCH, lq, Q_WIDTH), BF16),
        scratch_shapes=[pltpu.VMEM((rows, HEAD_DIM), BF16), pltpu.VMEM((1, rows), F32),
                        pltpu.VMEM((V_ROWS, rows), F32),
                        pltpu.VMEM((KEY_TILE, rows), F32), pltpu.VMEM((KEY_TILE, rows), F32)],
        compiler_params=_params(3),
        name=name,
    )(q, k, vt, km, vtm)


def _s5_project_in(xb, wb_ref, d, bu_ref):
    for ri in range(2):
        for ct in range(N_CH_TILES):
            bu_ref[ri, :, ct * STATE_TILE:(ct + 1) * STATE_TILE] = _dot(
                xb[:, ct * MXU_TILE:(ct + 1) * MXU_TILE], wb_ref[d, ri, ct])


def _s5_sweep(bu_ref, a_ref, d, lc, sr, si, emit):
    ar = a_ref[d, 0, :, lc:lc + SCAN_LANES]
    ai = a_ref[d, 1, :, lc:lc + SCAN_LANES]
    for step in range(CHUNK):
        i = step if d == 0 else CHUNK - 1 - step
        bur = bu_ref[0, i * SCAN_ROWS:(i + 1) * SCAN_ROWS, lc:lc + SCAN_LANES]
        bui = bu_ref[1, i * SCAN_ROWS:(i + 1) * SCAN_ROWS, lc:lc + SCAN_LANES]
        sr, si = ar * sr - ai * si + bur, ar * si + ai * sr + bui
        emit(i, sr, si)
    return sr, si


def _s5_local_body(x_ref, wb_ref, a_ref, sloc_ref, bu_ref):
    xb = x_ref[...].reshape(CHUNK * SCAN_ROWS, D_MODEL).astype(BF16)
    for d in range(2):
        _s5_project_in(xb, wb_ref, d, bu_ref)
        for lc in range(0, STATE_W, SCAN_LANES):
            zero = jnp.zeros((SCAN_ROWS, SCAN_LANES), F32)
            sr, si = _s5_sweep(bu_ref, a_ref, d, lc, zero, zero, lambda i, sr, si: None)
            sloc_ref[d, :, lc:lc + SCAN_LANES] = sr
            sloc_ref[d, :, STATE_W + lc:STATE_W + lc + SCAN_LANES] = si


def _s5_local(x3, wb, a):
    n_rows = x3.shape[1]
    return pl.pallas_call(
        _s5_local_body,
        grid=(n_rows // SCAN_ROWS,),
        in_specs=[pl.BlockSpec((CHUNK, SCAN_ROWS, D_MODEL), lambda r: (0, r, 0)), _const_spec(wb), _const_spec(a)],
        out_specs=pl.BlockSpec((2, SCAN_ROWS, 2 * STATE_W), lambda r: (0, r, 0)),
        out_shape=jax.ShapeDtypeStruct((2, n_rows, 2 * STATE_W), F32),
        scratch_shapes=[pltpu.VMEM((2, CHUNK * SCAN_ROWS, STATE_W), F32)],
        compiler_params=_params(1),
        name="s5_local",
    )(x3, wb, a)


def _s5_carry_body(sloc_ref, smeta_ref, a16_ref, sin_ref, sin_meta_ref, carry_ref):
    d, b, j = pl.program_id(0), pl.program_id(1), pl.program_id(2)
    n_blocks = pl.num_programs(2)
    fwd = d == 0

    @pl.when((d == 0) & (b == 0) & (j == 0))
    def _():
        sin_meta_ref[...] = jnp.zeros(sin_meta_ref.shape, F32)

    @pl.when(j == 0)
    def _():
        meta_state = smeta_ref[0, pl.ds(b, 1), :]
        carry_ref[0:1, :] = jnp.where(fwd, meta_state, jnp.zeros_like(meta_state))

    ar = a16_ref[pl.ds(d, 1), 0, 0, :][0:1]
    ai = a16_ref[pl.ds(d, 1), 1, 0, :][0:1]

    def row_step(r, _):
        row = jnp.where(fwd, r, CARRY_ROWS - 1 - r)
        c = carry_ref[0:1, :]
        sin_ref[0, pl.ds(row, 1), :] = c
        cr, ci = c[:, :STATE_W], c[:, STATE_W:]
        loc = sloc_ref[0, pl.ds(row, 1), :]
        carry_ref[0:1, :STATE_W] = ar * cr - ai * ci + loc[:, :STATE_W]
        carry_ref[0:1, STATE_W:] = ar * ci + ai * cr + loc[:, STATE_W:]
        return 0

    lax.fori_loop(0, CARRY_ROWS, row_step, 0)

    @pl.when((d == 1) & (j == n_blocks - 1))
    def _():
        sin_meta_ref[1, pl.ds(b, 1), :] = carry_ref[0:1, :]


def _s5_carry(sloc, sloc_meta, a16):
    n_blocks = N_CHUNKS // CARRY_ROWS

    def row_block(d, b, j):
        return (d, b * n_blocks + jnp.where(d == 0, j, n_blocks - 1 - j), 0)

    return pl.pallas_call(
        _s5_carry_body,
        grid=(2, BATCH, n_blocks),
        in_specs=[pl.BlockSpec((1, CARRY_ROWS, 2 * STATE_W), row_block),
                  pl.BlockSpec((1, SCAN_ROWS, 2 * STATE_W), lambda d, b, j: (0, 0, 0)),
                  _const_spec(a16)],
        out_specs=[pl.BlockSpec((1, CARRY_ROWS, 2 * STATE_W), row_block),
                   pl.BlockSpec((2, SCAN_ROWS, 2 * STATE_W), lambda d, b, j: (0, 0, 0))],
        out_shape=[jax.ShapeDtypeStruct(sloc.shape, F32), jax.ShapeDtypeStruct(sloc_meta.shape, F32)],
        scratch_shapes=[pltpu.VMEM((8, 2 * STATE_W), F32)],
        compiler_params=_params(3),
        name="s5_carry",
    )(sloc, sloc_meta, a16)


def _s5_output_body(x_ref, sin_ref, wb_ref, wc_ref, a_ref, y_ref, bu_ref, st_ref):
    xb = x_ref[...].reshape(CHUNK * SCAN_ROWS, D_MODEL).astype(BF16)
    y = None
    for d in range(2):
        _s5_project_in(xb, wb_ref, d, bu_ref)
        for lc in range(0, STATE_W, SCAN_LANES):
            def emit(i, sr, si, lc=lc):
                st_ref[0, i * SCAN_ROWS:(i + 1) * SCAN_ROWS, lc:lc + SCAN_LANES] = sr.astype(BF16)
                st_ref[1, i * SCAN_ROWS:(i + 1) * SCAN_ROWS, lc:lc + SCAN_LANES] = si.astype(BF16)
            _s5_sweep(bu_ref, a_ref, d, lc, sin_ref[d, :, lc:lc + SCAN_LANES],
                      sin_ref[d, :, STATE_W + lc:STATE_W + lc + SCAN_LANES], emit)
        cols = []
        for ct in range(N_CH_TILES):
            sl = slice(ct * STATE_TILE, (ct + 1) * STATE_TILE)
            cols.append(_dot(st_ref[0, :, sl], wc_ref[d, 0, ct]) + _dot(st_ref[1, :, sl], wc_ref[d, 1, ct]))
        yd = jnp.concatenate(cols, axis=1)
        y = yd if y is None else y + yd
    y_ref[...] = y.reshape(CHUNK, SCAN_ROWS, D_MODEL)


def _s5_output(x3, sin, wb, wc, a):
    n_rows = x3.shape[1]
    return pl.pallas_call(
        _s5_output_body,
        grid=(n_rows // SCAN_ROWS,),
        in_specs=[pl.BlockSpec((CHUNK, SCAN_ROWS, D_MODEL), lambda r: (0, r, 0)),
                  pl.BlockSpec((2, SCAN_ROWS, 2 * STATE_W), lambda r: (0, r, 0)),
                  _const_spec(wb), _const_spec(wc), _const_spec(a)],
        out_specs=pl.BlockSpec((CHUNK, SCAN_ROWS, D_MODEL), lambda r: (0, r, 0)),
        out_shape=jax.ShapeDtypeStruct(x3.shape, F32),
        scratch_shapes=[pltpu.VMEM((2, CHUNK * SCAN_ROWS, STATE_W), F32),
                        pltpu.VMEM((2, CHUNK * SCAN_ROWS, STATE_W), BF16)],
        compiler_params=_params(1),
        name="s5_output",
    )(x3, sin, wb, wc, a)


def _s5_weights(lam_re, lam_im, log_dt, b_re, b_im, c_re, c_im):
    dt = jnp.exp(log_dt)[..., None]
    mag = jnp.exp(lam_re * dt)
    abr = mag * jnp.cos(lam_im * dt)
    abi = mag * jnp.sin(lam_im * dt)
    nr, ni = abr - 1.0, abi
    den = lam_re * lam_re + lam_im * lam_im
    cr = (nr * lam_re + ni * lam_im) / den
    ci = (ni * lam_re - nr * lam_im) / den
    bbr = cr[..., None] * b_re - ci[..., None] * b_im
    bbi = cr[..., None] * b_im + ci[..., None] * b_re
    eye = jnp.eye(GROUPS_PER_TILE, dtype=F32)

    def blockdiag_in(m):
        m = m.reshape(2, N_CH_TILES, GROUPS_PER_TILE, S5_STATE, S5_GROUP_CH)
        w = jnp.einsum('dtgpc,gh->dtgchp', m, eye)
        return w.reshape(2, N_CH_TILES, MXU_TILE, STATE_TILE)

    def blockdiag_out(m):
        m = m.reshape(2, N_CH_TILES, GROUPS_PER_TILE, S5_GROUP_CH, S5_STATE)
        w = jnp.einsum('dtgcp,gh->dtgphc', m, eye)
        return w.reshape(2, N_CH_TILES, STATE_TILE, MXU_TILE)

    wb = jnp.stack([blockdiag_in(bbr), blockdiag_in(bbi)], axis=1).astype(BF16)
    wc = jnp.stack([blockdiag_out(c_re), blockdiag_out(-c_im)], axis=1).astype(BF16)
    a = jnp.stack([abr, abi], axis=1).reshape(2, 2, 1, STATE_W)
    pr, pi = abr, abi
    for _ in range(int(math.log2(CHUNK))):
        pr, pi = pr * pr - pi * pi, 2.0 * pr * pi
    a16 = jnp.stack([pr, pi], axis=1).reshape(2, 2, 1, STATE_W)
    return wb, wc, a, a16


def _s5_scan(hr, hm, wb, wc, a, a16):
    x3 = hr.reshape(CHUNK, CHUNK_ROWS, D_MODEL)
    xm3 = jnp.zeros((CHUNK, SCAN_ROWS, D_MODEL), F32).at[:, :BATCH].set(
        hm.reshape(BATCH, N_META, D_MODEL).transpose(1, 0, 2))
    sloc = _s5_local(x3, wb, a)
    sloc_meta = _s5_local(xm3, wb, a)
    sin, sin_meta = _s5_carry(sloc, sloc_meta, a16)
    y3 = _s5_output(x3, sin, wb, wc, a)
    ym3 = _s5_output(xm3, sin_meta, wb, wc, a)
    yr = y3.reshape(REAL_ROWS, D_MODEL)
    ym = ym3[:, :BATCH].transpose(1, 0, 2).reshape(META_ROWS, D_MODEL)
    return yr, ym


def _rope_tables():
    slot = jnp.arange(CHUNK, dtype=jnp.int32)[:, None, None]
    chunk = jnp.arange(N_CHUNKS, dtype=jnp.int32)[None, None, :]
    tok = jnp.broadcast_to(chunk * CHUNK + slot, (CHUNK, BATCH, N_CHUNKS)).reshape(-1)
    row_id = (tok // GRID_W).astype(F32)
    col_id = (tok % GRID_W).astype(F32)
    inv_freq = ROPE_THETA ** (-jnp.arange(0, ROPE_AXIS_DIM, 2, dtype=F32) / ROPE_AXIS_DIM)
    ang_r = row_id[:, None] * inv_freq[None, :]
    ang_c = col_id[:, None] * inv_freq[None, :]
    cos_r, sin_r, cos_c, sin_c = jnp.cos(ang_r), jnp.sin(ang_r), jnp.cos(ang_c), jnp.sin(ang_c)
    cos_head = jnp.concatenate([cos_r, cos_r, cos_c, cos_c], axis=1)
    sin_head = jnp.concatenate([-sin_r, sin_r, -sin_c, sin_c], axis=1)
    reps = LANES // HEAD_DIM
    cos, sin = jnp.tile(cos_head, (1, reps)), jnp.tile(sin_head, (1, reps))
    cos_meta = jnp.ones((META_ROWS, LANES), F32)
    sin_meta = jnp.zeros((META_ROWS, LANES), F32)
    return cos, sin, cos_meta, sin_meta


def kernel(x, meta_tokens, s5_lambda_re, s5_lambda_im, s5_log_dt, s5_b_re, s5_b_im, s5_c_re, s5_c_im, s5_d,
           s5_w_glu, s5_w_out, attn_w_qkv, attn_q_gain, attn_k_gain, attn_w_out, ffn_w_gate, ffn_w_up,
           ffn_w_down, ln_gain, ln_bias):
    hr = x.reshape(BATCH, N_CHUNKS, CHUNK, D_MODEL).transpose(2, 0, 1, 3).reshape(REAL_ROWS, D_MODEL)
    hm = jnp.broadcast_to(meta_tokens[None], (BATCH, N_META, D_MODEL)).reshape(META_ROWS, D_MODEL)
    cos, sin, cos_meta, sin_meta = _rope_tables()
    head_id = jnp.arange(LANES) // HEAD_DIM
    ones_blk = ((head_id[:, None] == head_id[None, :]).astype(F32) / HEAD_DIM).astype(BF16)

    def both(fn, *per_part):
        real = fn(*[p[0] for p in per_part], ROW_TILE, "real")
        meta = fn(*[p[1] for p in per_part], N_META, "meta")
        return real, meta

    for i in range(DEPTH):
        j = i // 2
        gain0, bias0 = ln_gain[i, 0][None], ln_bias[i, 0][None]
        gain1, bias1 = ln_gain[i, 1][None], ln_bias[i, 1][None]
        if i % 2 == 0:
            wb, wc, a, a16 = _s5_weights(s5_lambda_re[j], s5_lambda_im[j], s5_log_dt[j], s5_b_re[j], s5_b_im[j],
                                         s5_c_re[j], s5_c_im[j])
            yr, ym = _s5_scan(hr, hm, wb, wc, a, a16)
            d, wglu, wout = s5_d[j][None], s5_w_glu[j].astype(BF16), s5_w_out[j].astype(BF16)
            hr, hm = both(lambda h, y, tile, tag: _s5_post(h, y, d, wglu, wout, gain0, bias0, tile, f"s5_post_{tag}"),
                          (hr, hm), (yr, ym))
        else:
            wqkv, wout = attn_w_qkv[j].astype(BF16), attn_w_out[j].astype(BF16)
            qg = jnp.tile(attn_q_gain[j], N_Q_HEADS)[None]
            kg = jnp.tile(attn_k_gain[j], N_KV_HEADS)[None]
            q, k, vt = _qkv(hr, wqkv, qg, kg, cos, sin, ones_blk, ROW_TILE, "qkv_real", True)
            qm, km, vm = _qkv(hm, wqkv, qg, kg, cos_meta, sin_meta, ones_blk, N_META, "qkv_meta", False)
            vtm = jnp.concatenate([vm.swapaxes(2, 3), jnp.ones((BATCH, N_KV_HEADS, V_ROWS - HEAD_DIM, N_META), BF16)],
                                  axis=2)
            o = _flash(q, k, vt, km, vtm, Q_TILE, "flash_real")
            om = _flash(qm, k, vt, km, vtm, N_META, "flash_meta")
            hr, hm = both(lambda h, oo, tile, tag: _attn_out(h, oo, wout, gain0, bias0, tile, f"attn_out_{tag}"),
                          (hr, hm), (o, om))
        wg, wu, wd = ffn_w_gate[i].astype(BF16), ffn_w_up[i].astype(BF16), ffn_w_down[i].astype(BF16)
        hr, hm = both(lambda h, tile, tag: _ffn(h, wg, wu, wd, gain1, bias1, tile, f"ffn_{tag}"), (hr, hm))
    return hr.reshape(CHUNK, BATCH, N_CHUNKS, D_MODEL).transpose(1, 2, 0, 3).reshape(BATCH, SEQ, D_MODEL)
```

```python
import functools
import math

import jax
import jax.numpy as jnp
from jax import lax
from jax.experimental import pallas as pl
from jax.experimental.pallas import tpu as pltpu

D_MODEL = 1024
BATCH = 2
SEQ = 8192
DEPTH = 4
N_META = 16
GRID_W = 64
HEAD_DIM = 64
N_Q_HEADS = D_MODEL // HEAD_DIM
N_KV_HEADS = N_Q_HEADS // 4
Q_PER_KV = N_Q_HEADS // N_KV_HEADS
Q_WIDTH = N_Q_HEADS * HEAD_DIM
KV_WIDTH = N_KV_HEADS * HEAD_DIM
ROPE_THETA = 10000.0
ROPE_AXIS_DIM = HEAD_DIM // 2
QK_EPS = 1e-6
S5_GROUP_CH = 16
S5_GROUPS = D_MODEL // S5_GROUP_CH
S5_STATE = 64
D_FF = -(-8 * D_MODEL // (3 * 256)) * 256
LN_EPS = 1e-5
DEEPNORM_ALPHA = (2.0 * DEPTH) ** 0.25

LANES = 128
MXU_TILE = 256
CHUNK = 16
N_CHUNKS = SEQ // CHUNK
REAL_ROWS = BATCH * SEQ
CHUNK_ROWS = BATCH * N_CHUNKS
META_ROWS = BATCH * N_META
ROW_TILE = 512
SCAN_ROWS = 16
STATE_W = S5_GROUPS * S5_STATE
GROUPS_PER_TILE = MXU_TILE // S5_GROUP_CH
STATE_TILE = GROUPS_PER_TILE * S5_STATE
N_CH_TILES = D_MODEL // MXU_TILE
SCAN_LANES = 256
Q_TILE = 256
KEY_TILE = 1024
V_ROWS = 2 * HEAD_DIM
LOG2_E = math.log2(math.e)
VMEM_LIMIT = 56 * 1024 * 1024

F32 = jnp.float32
BF16 = jnp.bfloat16


def _const_spec(a):
    return pl.BlockSpec(a.shape, lambda *_, nd=a.ndim: (0,) * nd, pipeline_mode=pl.Buffered(1))


def _params(n_axes):
    return pltpu.CompilerParams(dimension_semantics=("arbitrary",) * n_axes, vmem_limit_bytes=VMEM_LIMIT)


def _dot(a, b):
    return jnp.dot(a, b, preferred_element_type=F32)


def _layer_norm(z, gain, bias):
    mean = jnp.mean(z, axis=-1, keepdims=True)
    zc = z - mean
    var = jnp.mean(zc * zc, axis=-1, keepdims=True)
    return zc * lax.rsqrt(var + LN_EPS) * gain + bias


def _rowwise_call(body, row_inputs, const_inputs, out_dtypes, tile, name):
    rows = row_inputs[0].shape[0]
    row_spec = lambda a: pl.BlockSpec((tile, a.shape[1]), lambda i: (i, 0))
    return pl.pallas_call(
        body,
        grid=(rows // tile,),
        in_specs=[row_spec(a) for a in row_inputs] + [_const_spec(a) for a in const_inputs],
        out_specs=[pl.BlockSpec((tile, D_MODEL), lambda i: (i, 0)) for _ in out_dtypes],
        out_shape=[jax.ShapeDtypeStruct((rows, D_MODEL), dt) for dt in out_dtypes],
        compiler_params=_params(1),
        name=name,
    )(*row_inputs, *const_inputs)


def _ffn_body(h_ref, wg_ref, wu_ref, wd_ref, gain_ref, bias_ref, o_ref):
    h = h_ref[...]
    hb = h.astype(BF16)
    gate = _dot(hb, wg_ref[...])
    up = _dot(hb, wu_ref[...])
    act = (gate * jax.nn.sigmoid(gate) * up).astype(BF16)
    z = DEEPNORM_ALPHA * h + _dot(act, wd_ref[...])
    o_ref[...] = _layer_norm(z, gain_ref[...], bias_ref[...])


def _ffn(h, wg, wu, wd, gain, bias, tile, name):
    return _rowwise_call(_ffn_body, [h], [wg, wu, wd, gain, bias], [F32], tile, name)[0]


def _s5_post_body(h_ref, yf_ref, yr_ref, d_ref, wglu_ref, wout_ref, gain_ref, bias_ref, o_ref):
    h = h_ref[...]
    y = yf_ref[...] + yr_ref[...] + d_ref[...] * h
    g = 0.5 * y * (1.0 + lax.erf(y * math.sqrt(0.5)))
    gb = g.astype(BF16)
    zz = (g * jax.nn.sigmoid(_dot(gb, wglu_ref[...]))).astype(BF16)
    z = DEEPNORM_ALPHA * h + _dot(zz, wout_ref[...])
    o_ref[...] = _layer_norm(z, gain_ref[...], bias_ref[...])


def _s5_post(h, yf, yr, d, wglu, wout, gain, bias, tile, name):
    return _rowwise_call(_s5_post_body, [h, yf, yr], [d, wglu, wout, gain, bias], [F32], tile, name)[0]


def _attn_out_body(h_ref, o_ref, wout_ref, gain_ref, bias_ref, out_ref):
    h = h_ref[...]
    z = DEEPNORM_ALPHA * h + _dot(o_ref[0], wout_ref[...])
    out_ref[...] = _layer_norm(z, gain_ref[...], bias_ref[...])


def _attn_out(h, o, wout, gain, bias, tile, name):
    rows = h.shape[0]
    consts = [wout, gain, bias]
    return pl.pallas_call(
        _attn_out_body,
        grid=(rows // tile,),
        in_specs=[pl.BlockSpec((tile, D_MODEL), lambda t: (t, 0)),
                  pl.BlockSpec((1, tile, D_MODEL), lambda t: (t % BATCH, t // BATCH, 0))]
        + [_const_spec(a) for a in consts],
        out_specs=pl.BlockSpec((tile, D_MODEL), lambda t: (t, 0)),
        out_shape=jax.ShapeDtypeStruct((rows, D_MODEL), F32),
        compiler_params=_params(1),
        name=name,
    )(h, o, *consts)


def _rms_rope(t, gain, cos, sin, ones_blk):
    lane = lax.broadcasted_iota(jnp.int32, (t.shape[0], LANES), 1)
    first_half = (lane % ROPE_AXIS_DIM) < (ROPE_AXIS_DIM // 2)
    cols = []
    for c in range(t.shape[1] // LANES):
        tc = t[:, c * LANES:(c + 1) * LANES]
        sq = tc * tc
        hi = sq.astype(BF16)
        lo = (sq - hi.astype(F32)).astype(BF16)
        ms = _dot(hi, ones_blk) + _dot(lo, ones_blk)
        tn = tc * lax.rsqrt(ms + QK_EPS) * gain[:, c * LANES:(c + 1) * LANES]
        half = ROPE_AXIS_DIM // 2
        partner = jnp.where(first_half, pltpu.roll(tn, LANES - half, axis=1), pltpu.roll(tn, half, axis=1))
        cols.append(tn * cos + partner * sin)
    return jnp.concatenate(cols, axis=1)


def _qkv_body(h_ref, w_ref, qg_ref, kg_ref, cos_ref, sin_ref, ones_ref, q_ref, k_ref, v_ref, *, v_transposed):
    qkv = _dot(h_ref[...].astype(BF16), w_ref[...])
    cos, sin, ones_blk = cos_ref[...], sin_ref[...], ones_ref[...]
    q = _rms_rope(qkv[:, :Q_WIDTH], qg_ref[...], cos, sin, ones_blk)
    k = _rms_rope(qkv[:, Q_WIDTH:Q_WIDTH + KV_WIDTH], kg_ref[...], cos, sin, ones_blk)
    v = qkv[:, Q_WIDTH + KV_WIDTH:]
    q_ref[0] = (q * (HEAD_DIM ** -0.5 * LOG2_E)).astype(BF16)
    if v_transposed:
        vt = v.T
    for g in range(N_KV_HEADS):
        k_ref[0, g] = k[:, g * HEAD_DIM:(g + 1) * HEAD_DIM].astype(BF16)
        if v_transposed:
            v_ref[0, g, 0:HEAD_DIM, :] = vt[g * HEAD_DIM:(g + 1) * HEAD_DIM, :].astype(BF16)
            v_ref[0, g, HEAD_DIM:, :] = jnp.ones((V_ROWS - HEAD_DIM, vt.shape[1]), BF16)
        else:
            v_ref[0, g] = v[:, g * HEAD_DIM:(g + 1) * HEAD_DIM].astype(BF16)


def _qkv(h, w, qg, kg, cos, sin, ones_blk, tile, name, v_transposed):
    rows = h.shape[0]
    per_batch = rows // BATCH
    consts_a = [w, qg, kg]
    if v_transposed:
        v_spec = pl.BlockSpec((1, N_KV_HEADS, V_ROWS, tile), lambda t: (t % BATCH, 0, 0, t // BATCH))
        v_shape = jax.ShapeDtypeStruct((BATCH, N_KV_HEADS, V_ROWS, per_batch), BF16)
    else:
        v_spec = pl.BlockSpec((1, N_KV_HEADS, tile, HEAD_DIM), lambda t: (t % BATCH, 0, t // BATCH, 0))
        v_shape = jax.ShapeDtypeStruct((BATCH, N_KV_HEADS, per_batch, HEAD_DIM), BF16)
    return pl.pallas_call(
        functools.partial(_qkv_body, v_transposed=v_transposed),
        grid=(rows // tile,),
        in_specs=[pl.BlockSpec((tile, D_MODEL), lambda t: (t, 0))] + [_const_spec(a) for a in consts_a]
        + [pl.BlockSpec((tile, LANES), lambda t: (t, 0)), pl.BlockSpec((tile, LANES), lambda t: (t, 0)),
           _const_spec(ones_blk)],
        out_specs=[pl.BlockSpec((1, tile, Q_WIDTH), lambda t: (t % BATCH, t // BATCH, 0)),
                   pl.BlockSpec((1, N_KV_HEADS, tile, HEAD_DIM), lambda t: (t % BATCH, 0, t // BATCH, 0)),
                   v_spec],
        out_shape=[jax.ShapeDtypeStruct((BATCH, per_batch, Q_WIDTH), BF16),
                   jax.ShapeDtypeStruct((BATCH, N_KV_HEADS, per_batch, HEAD_DIM), BF16),
                   v_shape],
        compiler_params=_params(1),
        name=name,
    )(h, *consts_a, cos, sin, ones_blk)


def _flash_body(q_ref, k_ref, vt_ref, km_ref, vtm_ref, o_ref, qs_ref, m_ref, acc_ref, sa_ref, sb_ref, *,
                tq, n_key_tiles):
    qb = q_ref[0]
    for h in range(Q_PER_KV):
        qs_ref[h * tq:(h + 1) * tq, :] = qb[:, h * HEAD_DIM:(h + 1) * HEAD_DIM]
    m_ref[...] = jnp.full(m_ref.shape, -jnp.inf, F32)
    acc_ref[...] = jnp.zeros(acc_ref.shape, F32)

    def scores(k):
        return lax.dot_general(k, qs_ref[...], (((1,), (1,)), ((), ())), preferred_element_type=F32)

    def absorb(st, vt):
        m_prev = m_ref[...]
        m_new = jnp.maximum(m_prev, jnp.max(st, axis=0, keepdims=True))
        alpha = jnp.exp2(m_prev - m_new)
        pt = jnp.exp2(st - m_new).astype(BF16)
        acc_ref[...] = alpha * acc_ref[...] + _dot(vt, pt)
        m_ref[...] = m_new

    def k_tile(j):
        return k_ref[0, 0, pl.ds(pl.multiple_of(j * KEY_TILE, KEY_TILE), KEY_TILE), :]

    def v_tile(j):
        return vt_ref[0, 0, :, pl.ds(pl.multiple_of(j * KEY_TILE, KEY_TILE), KEY_TILE)]

    sa_ref[...] = scores(k_tile(0))
    absorb(scores(km_ref[0, 0]), vtm_ref[0, 0])

    def pair(jj, last):
        sb_ref[...] = scores(k_tile(2 * jj + 1))
        absorb(sa_ref[...], v_tile(2 * jj))
        if not last:
            sa_ref[...] = scores(k_tile(2 * jj + 2))
        absorb(sb_ref[...], v_tile(2 * jj + 1))

    def pair_step(jj, carry):
        pair(jj, False)
        return carry

    n_pairs = n_key_tiles // 2
    lax.fori_loop(0, n_pairs - 1, pair_step, 0)
    pair(n_pairs - 1, True)
    acc = acc_ref[...]
    out = (acc[:HEAD_DIM, :] / acc[HEAD_DIM:HEAD_DIM + 1, :]).T
    o_ref[0] = jnp.concatenate([out[h * tq:(h + 1) * tq, :] for h in range(Q_PER_KV)], axis=1).astype(BF16)


def _flash(q, k, vt, km, vtm, tq, name):
    lq = q.shape[1]
    rows = Q_PER_KV * tq
    head = lambda b, g, t: (b, g, 0, 0)
    return pl.pallas_call(
        functools.partial(_flash_body, tq=tq, n_key_tiles=SEQ // KEY_TILE),
        grid=(BATCH, N_KV_HEADS, lq // tq),
        in_specs=[pl.BlockSpec((1, tq, Q_PER_KV * HEAD_DIM), lambda b, g, t: (b, t, g)),
                  pl.BlockSpec((1, 1, SEQ, HEAD_DIM), head), pl.BlockSpec((1, 1, V_ROWS, SEQ), head),
                  pl.BlockSpec((1, 1, N_META, HEAD_DIM), head), pl.BlockSpec((1, 1, V_ROWS, N_META), head)],
        out_specs=pl.BlockSpec((1, tq, Q_PER_KV * HEAD_DIM), lambda b, g, t: (b, t, g)),
        out_shape=jax.ShapeDtypeStruct((BATCH, lq, Q_WIDTH), BF16),
        scratch_shapes=[pltpu.VMEM((rows, HEAD_DIM), BF16), pltpu.VMEM((1, rows), F32),
                        pltpu.VMEM((V_ROWS, rows), F32),
                        pltpu.VMEM((KEY_TILE, rows), F32), pltpu.VMEM((KEY_TILE, rows), F32)],
        compiler_params=_params(3),
        name=name,
    )(q, k, vt, km, vtm)


def _complex_axpy(ar, ai, sr, si, br, bi):
    return ar * sr - ai * si + br, ar * si + ai * sr + bi


def _s5_sweep(bu_ref, a_ref, order, lc, sr, si, emit):
    ar = a_ref[0, :, lc:lc + SCAN_LANES]
    ai = a_ref[1, :, lc:lc + SCAN_LANES]
    for i in order:
        bur = bu_ref[0, i * SCAN_ROWS:(i + 1) * SCAN_ROWS, lc:lc + SCAN_LANES]
        bui = bu_ref[1, i * SCAN_ROWS:(i + 1) * SCAN_ROWS, lc:lc + SCAN_LANES]
        sr, si = _complex_axpy(ar, ai, sr, si, bur, bui)
        emit(i, sr, si)
    return sr, si


def _s5_dir_body(x_ref, init_ref, wb_ref, wc_ref, a_ref, a16_ref, y_ref, end_ref,
                 bu_ref, st_ref, loc_ref, sin_ref, carry_ref, *, reverse, chain_rows):
    b, j = pl.program_id(0), pl.program_id(1)
    order = range(CHUNK - 1, -1, -1) if reverse else range(CHUNK)
    if chain_rows:
        @pl.when(j == 0)
        def _():
            carry_ref[0:1, :] = init_ref[pl.ds(b, 1), :]

    xb = x_ref[...].reshape(CHUNK * SCAN_ROWS, D_MODEL).astype(BF16)
    for ct in range(N_CH_TILES):
        re = slice(ct * STATE_TILE, (ct + 1) * STATE_TILE)
        im = slice(STATE_W + ct * STATE_TILE, STATE_W + (ct + 1) * STATE_TILE)
        xct = xb[:, ct * MXU_TILE:(ct + 1) * MXU_TILE]
        bu_ref[0, :, re] = _dot(xct, wb_ref[0, ct])
        bu_ref[1, :, re] = _dot(xct, wb_ref[1, ct])

        for lc in range(re.start, re.stop, SCAN_LANES):
            zero = jnp.zeros((SCAN_ROWS, SCAN_LANES), F32)
            sr, si = _s5_sweep(bu_ref, a_ref, order, lc, zero, zero, lambda i, sr, si: None)
            loc_ref[:, lc:lc + SCAN_LANES] = sr
            loc_ref[:, STATE_W + lc:STATE_W + lc + SCAN_LANES] = si

        pr, pi = a16_ref[0, :, re], a16_ref[1, :, re]
        if chain_rows:
            cr, ci = carry_ref[0:1, re], carry_ref[0:1, im]
            for r in order:
                sin_ref[r:r + 1, re] = cr
                sin_ref[r:r + 1, im] = ci
                cr, ci = _complex_axpy(pr, pi, cr, ci, loc_ref[r:r + 1, re], loc_ref[r:r + 1, im])
            carry_ref[0:1, re] = cr
            carry_ref[0:1, im] = ci
            end_ref[0, :, re] = jnp.broadcast_to(cr, (end_ref.shape[1], STATE_TILE))
            end_ref[0, :, im] = jnp.broadcast_to(ci, (end_ref.shape[1], STATE_TILE))
        else:
            sr, si = init_ref[:, re], init_ref[:, im]
            sin_ref[:, re] = sr
            sin_ref[:, im] = si
            er, ei = _complex_axpy(pr, pi, sr, si, loc_ref[:, re], loc_ref[:, im])
            end_ref[:, re] = er
            end_ref[:, im] = ei

        for lc in range(re.start, re.stop, SCAN_LANES):
            def emit(i, sr, si, lc=lc):
                st_ref[0, i * SCAN_ROWS:(i + 1) * SCAN_ROWS, lc:lc + SCAN_LANES] = sr.astype(BF16)
                st_ref[1, i * SCAN_ROWS:(i + 1) * SCAN_ROWS, lc:lc + SCAN_LANES] = si.astype(BF16)
            _s5_sweep(bu_ref, a_ref, order, lc, sin_ref[:, lc:lc + SCAN_LANES],
                      sin_ref[:, STATE_W + lc:STATE_W + lc + SCAN_LANES], emit)

        yct = _dot(st_ref[0, :, re], wc_ref[0, ct]) + _dot(st_ref[1, :, re], wc_ref[1, ct])
        y_ref[:, :, ct * MXU_TILE:(ct + 1) * MXU_TILE] = yct.reshape(CHUNK, SCAN_ROWS, MXU_TILE)


def _s5_dir(x3, init, wb, wc, a, a16, reverse, chain_rows, name):
    n_rows = x3.shape[1]
    if chain_rows:
        n_batch, n_blocks = BATCH, n_rows // (BATCH * SCAN_ROWS)
        end_shape, end_spec = (BATCH, 8, 2 * STATE_W), pl.BlockSpec((1, 8, 2 * STATE_W), lambda b, j: (b, 0, 0))
    else:
        n_batch, n_blocks = 1, n_rows // SCAN_ROWS
        end_shape, end_spec = init.shape, pl.BlockSpec(init.shape, lambda b, j: (0, 0))

    def row_block(b, j):
        return (0, b * n_blocks + (n_blocks - 1 - j if reverse else j), 0)

    consts = [init, wb, wc, a, a16]
    return pl.pallas_call(
        functools.partial(_s5_dir_body, reverse=reverse, chain_rows=chain_rows),
        grid=(n_batch, n_blocks),
        in_specs=[pl.BlockSpec((CHUNK, SCAN_ROWS, D_MODEL), row_block)] + [_const_spec(c) for c in consts],
        out_specs=[pl.BlockSpec((CHUNK, SCAN_ROWS, D_MODEL), row_block), end_spec],
        out_shape=[jax.ShapeDtypeStruct(x3.shape, F32), jax.ShapeDtypeStruct(end_shape, F32)],
        scratch_shapes=[pltpu.VMEM((2, CHUNK * SCAN_ROWS, STATE_W), F32),
                        pltpu.VMEM((2, CHUNK * SCAN_ROWS, STATE_W), BF16),
                        pltpu.VMEM((SCAN_ROWS, 2 * STATE_W), F32),
                        pltpu.VMEM((SCAN_ROWS, 2 * STATE_W), F32),
                        pltpu.VMEM((8, 2 * STATE_W), F32)],
        compiler_params=_params(2),
        name=name,
    )(x3, *consts)


def _s5_weights(lam_re, lam_im, log_dt, b_re, b_im, c_re, c_im):
    dt = jnp.exp(log_dt)[..., None]
    mag = jnp.exp(lam_re * dt)
    abr = mag * jnp.cos(lam_im * dt)
    abi = mag * jnp.sin(lam_im * dt)
    nr, ni = abr - 1.0, abi
    den = lam_re * lam_re + lam_im * lam_im
    cr = (nr * lam_re + ni * lam_im) / den
    ci = (ni * lam_re - nr * lam_im) / den
    bbr = cr[..., None] * b_re - ci[..., None] * b_im
    bbi = cr[..., None] * b_im + ci[..., None] * b_re
    ch_group = jnp.arange(MXU_TILE) // S5_GROUP_CH
    st_group = jnp.arange(STATE_TILE) // S5_STATE

    def blockdiag_in(m):
        m = m.reshape(2, N_CH_TILES, GROUPS_PER_TILE, S5_STATE, S5_GROUP_CH)
        row = m.transpose(0, 1, 4, 2, 3).reshape(2, N_CH_TILES, 1, S5_GROUP_CH, STATE_TILE)
        full = jnp.broadcast_to(row, (2, N_CH_TILES, GROUPS_PER_TILE, S5_GROUP_CH, STATE_TILE))
        full = full.reshape(2, N_CH_TILES, MXU_TILE, STATE_TILE)
        return jnp.where(ch_group[:, None] == st_group[None, :], full, 0.0)

    def blockdiag_out(m):
        m = m.reshape(2, N_CH_TILES, GROUPS_PER_TILE, S5_GROUP_CH, S5_STATE)
        col = m.transpose(0, 1, 2, 4, 3).reshape(2, N_CH_TILES, STATE_TILE, 1, S5_GROUP_CH)
        full = jnp.broadcast_to(col, (2, N_CH_TILES, STATE_TILE, GROUPS_PER_TILE, S5_GROUP_CH))
        full = full.reshape(2, N_CH_TILES, STATE_TILE, MXU_TILE)
        return jnp.where(st_group[:, None] == ch_group[None, :], full, 0.0)

    wb = jnp.stack([blockdiag_in(bbr), blockdiag_in(bbi)], axis=1).astype(BF16)
    wc = jnp.stack([blockdiag_out(c_re), blockdiag_out(-c_im)], axis=1).astype(BF16)
    a = jnp.stack([abr, abi], axis=1).reshape(2, 2, 1, STATE_W)
    pr, pi = abr, abi
    for _ in range(int(math.log2(CHUNK))):
        pr, pi = pr * pr - pi * pi, 2.0 * pr * pi
    a16 = jnp.stack([pr, pi], axis=1).reshape(2, 2, 1, STATE_W)
    return wb, wc, a, a16


def _s5_scan(hr, hm, wb, wc, a, a16):
    x3 = hr.reshape(CHUNK, CHUNK_ROWS, D_MODEL)
    xm3 = jnp.zeros((CHUNK, SCAN_ROWS, D_MODEL), F32).at[:, :BATCH].set(
        hm.reshape(BATCH, N_META, D_MODEL).transpose(1, 0, 2))
    zero_state = jnp.zeros((SCAN_ROWS, 2 * STATE_W), F32)
    fwd = lambda x, init, chain, name: _s5_dir(x, init, wb[0], wc[0], a[0], a16[0], False, chain, name)
    rev = lambda x, init, chain, name: _s5_dir(x, init, wb[1], wc[1], a[1], a16[1], True, chain, name)
    ymf3, meta_end = fwd(xm3, zero_state, False, "s5_fwd_meta")
    yf3, _ = fwd(x3, meta_end, True, "s5_fwd_real")
    yr3, real_end = rev(x3, zero_state, True, "s5_rev_real")
    ymr3, _ = rev(xm3, zero_state.at[:BATCH].set(real_end[:, 0]), False, "s5_rev_meta")
    flat_meta = lambda y3: y3[:, :BATCH].transpose(1, 0, 2).reshape(META_ROWS, D_MODEL)
    return ((yf3.reshape(REAL_ROWS, D_MODEL), yr3.reshape(REAL_ROWS, D_MODEL)),
            (flat_meta(ymf3), flat_meta(ymr3)))


def _rope_tables():
    slot = jnp.arange(CHUNK, dtype=jnp.int32)[:, None, None]
    chunk = jnp.arange(N_CHUNKS, dtype=jnp.int32)[None, None, :]
    tok = jnp.broadcast_to(chunk * CHUNK + slot, (CHUNK, BATCH, N_CHUNKS)).reshape(-1)
    row_id = (tok // GRID_W).astype(F32)
    col_id = (tok % GRID_W).astype(F32)
    inv_freq = ROPE_THETA ** (-jnp.arange(0, ROPE_AXIS_DIM, 2, dtype=F32) / ROPE_AXIS_DIM)
    ang_r = row_id[:, None] * inv_freq[None, :]
    ang_c = col_id[:, None] * inv_freq[None, :]
    cos_r, sin_r, cos_c, sin_c = jnp.cos(ang_r), jnp.sin(ang_r), jnp.cos(ang_c), jnp.sin(ang_c)
    cos_head = jnp.concatenate([cos_r, cos_r, cos_c, cos_c], axis=1)
    sin_head = jnp.concatenate([-sin_r, sin_r, -sin_c, sin_c], axis=1)
    reps = LANES // HEAD_DIM
    cos, sin = jnp.tile(cos_head, (1, reps)), jnp.tile(sin_head, (1, reps))
    cos_meta = jnp.ones((META_ROWS, LANES), F32)
    sin_meta = jnp.zeros((META_ROWS, LANES), F32)
    return cos, sin, cos_meta, sin_meta


def kernel(x, meta_tokens, s5_lambda_re, s5_lambda_im, s5_log_dt, s5_b_re, s5_b_im, s5_c_re, s5_c_im, s5_d,
           s5_w_glu, s5_w_out, attn_w_qkv, attn_q_gain, attn_k_gain, attn_w_out, ffn_w_gate, ffn_w_up,
           ffn_w_down, ln_gain, ln_bias):
    hr = x.reshape(BATCH, N_CHUNKS, CHUNK, D_MODEL).transpose(2, 0, 1, 3).reshape(REAL_ROWS, D_MODEL)
    hm = jnp.broadcast_to(meta_tokens[None], (BATCH, N_META, D_MODEL)).reshape(META_ROWS, D_MODEL)
    cos, sin, cos_meta, sin_meta = _rope_tables()
    head_id = jnp.arange(LANES) // HEAD_DIM
    ones_blk = ((head_id[:, None] == head_id[None, :]).astype(F32) / HEAD_DIM).astype(BF16)

    def both(fn, *per_part):
        real = fn(*[p[0] for p in per_part], ROW_TILE, "real")
        meta = fn(*[p[1] for p in per_part], N_META, "meta")
        return real, meta

    for i in range(DEPTH):
        j = i // 2
        gain0, bias0 = ln_gain[i, 0][None], ln_bias[i, 0][None]
        gain1, bias1 = ln_gain[i, 1][None], ln_bias[i, 1][None]
        if i % 2 == 0:
            wb, wc, a, a16 = _s5_weights(s5_lambda_re[j], s5_lambda_im[j], s5_log_dt[j], s5_b_re[j], s5_b_im[j],
                                         s5_c_re[j], s5_c_im[j])
            (yf, yr), (ymf, ymr) = _s5_scan(hr, hm, wb, wc, a, a16)
            d, wglu, wout = s5_d[j][None], s5_w_glu[j].astype(BF16), s5_w_out[j].astype(BF16)
            hr, hm = both(
                lambda h, y1, y2, tile, tag: _s5_post(h, y1, y2, d, wglu, wout, gain0, bias0, tile, f"s5_post_{tag}"),
                (hr, hm), (yf, ymf), (yr, ymr))
        else:
            wqkv, wout = attn_w_qkv[j].astype(BF16), attn_w_out[j].astype(BF16)
            qg = jnp.tile(attn_q_gain[j], N_Q_HEADS)[None]
            kg = jnp.tile(attn_k_gain[j], N_KV_HEADS)[None]
            q, k, vt = _qkv(hr, wqkv, qg, kg, cos, sin, ones_blk, ROW_TILE, "qkv_real", True)
            qm, km, vm = _qkv(hm, wqkv, qg, kg, cos_meta, sin_meta, ones_blk, N_META, "qkv_meta", False)
            vtm = jnp.concatenate([vm.swapaxes(2, 3), jnp.ones((BATCH, N_KV_HEADS, V_ROWS - HEAD_DIM, N_META), BF16)],
                                  axis=2)
            o = _flash(q, k, vt, km, vtm, Q_TILE, "flash_real")
            om = _flash(qm, k, vt, km, vtm, N_META, "flash_meta")
            hr, hm = both(lambda h, oo, tile, tag: _attn_out(h, oo, wout, gain0, bias0, tile, f"attn_out_{tag}"),
                          (hr, hm), (o, om))
        wg, wu, wd = ffn_w_gate[i].astype(BF16), ffn_w_up[i].astype(BF16), ffn_w_down[i].astype(BF16)
        hr, hm = both(lambda h, tile, tag: _ffn(h, wg, wu, wd, gain1, bias1, tile, f"ffn_{tag}"), (hr, hm))
    return hr.reshape(CHUNK, BATCH, N_CHUNKS, D_MODEL).transpose(1, 2, 0, 3).reshape(BATCH, SEQ, D_MODEL)
```

```python
import functools
import math

import jax
import jax.numpy as jnp
from jax import lax
from jax.experimental import pallas as pl
from jax.experimental.pallas import tpu as pltpu

D_MODEL = 1024
BATCH = 2
SEQ = 8192
DEPTH = 4
N_META = 16
GRID_W = 64
HEAD_DIM = 64
N_Q_HEADS = D_MODEL // HEAD_DIM
N_KV_HEADS = N_Q_HEADS // 4
Q_PER_KV = N_Q_HEADS // N_KV_HEADS
Q_WIDTH = N_Q_HEADS * HEAD_DIM
KV_WIDTH = N_KV_HEADS * HEAD_DIM
ROPE_THETA = 10000.0
ROPE_AXIS_DIM = HEAD_DIM // 2
QK_EPS = 1e-6
S5_GROUP_CH = 16
S5_GROUPS = D_MODEL // S5_GROUP_CH
S5_STATE = 64
D_FF = -(-8 * D_MODEL // (3 * 256)) * 256
LN_EPS = 1e-5
DEEPNORM_ALPHA = (2.0 * DEPTH) ** 0.25

LANES = 128
MXU_TILE = 256
CHUNK = 16
N_CHUNKS = SEQ // CHUNK
REAL_ROWS = BATCH * SEQ
CHUNK_ROWS = BATCH * N_CHUNKS
META_ROWS = BATCH * N_META
ROW_TILE = 512
SCAN_ROWS = 16
STATE_W = S5_GROUPS * S5_STATE
GROUPS_PER_TILE = MXU_TILE // S5_GROUP_CH
STATE_TILE = GROUPS_PER_TILE * S5_STATE
N_CH_TILES = D_MODEL // MXU_TILE
SCAN_LANES = 256
Q_TILE = 256
KEY_TILE = 1024
V_ROWS = 2 * HEAD_DIM
LOG2_E = math.log2(math.e)
VMEM_LIMIT = 56 * 1024 * 1024

F32 = jnp.float32
BF16 = jnp.bfloat16


def _const_spec(a):
    return pl.BlockSpec(a.shape, lambda *_, nd=a.ndim: (0,) * nd, pipeline_mode=pl.Buffered(1))


def _params(n_axes):
    return pltpu.CompilerParams(dimension_semantics=("arbitrary",) * n_axes, vmem_limit_bytes=VMEM_LIMIT)


def _dot(a, b):
    return jnp.dot(a, b, preferred_element_type=F32)


def _layer_norm(z, gain, bias):
    mean = jnp.mean(z, axis=-1, keepdims=True)
    zc = z - mean
    var = jnp.mean(zc * zc, axis=-1, keepdims=True)
    return zc * lax.rsqrt(var + LN_EPS) * gain + bias


def _s5_mix(h, yf, yr, d_ref, wglu_ref, wout_ref):
    y = yf + yr + d_ref[...] * h
    g = 0.5 * y * (1.0 + lax.erf(y * math.sqrt(0.5)))
    zz = (g * jax.nn.sigmoid(_dot(g.astype(BF16), wglu_ref[...]))).astype(BF16)
    return _dot(zz, wout_ref[...])


def _attn_mix(h, o, wout_ref):
    return _dot(o, wout_ref[...])


def _residual_layer(mix, n_mix_consts, vals, const_refs):
    mix_refs = const_refs[:n_mix_consts]
    gain0, bias0, wg, wu, wd, gain1, bias1 = const_refs[n_mix_consts:]
    h = vals[0]
    h = _layer_norm(DEEPNORM_ALPHA * h + mix(*vals, *mix_refs), gain0[...], bias0[...])
    hb = h.astype(BF16)
    gate = _dot(hb, wg[...])
    up = _dot(hb, wu[...])
    act = (gate * jax.nn.sigmoid(gate) * up).astype(BF16)
    return _layer_norm(DEEPNORM_ALPHA * h + _dot(act, wd[...]), gain1[...], bias1[...])


def _residual_layer_body(*refs, mix, n_rows, n_mix_consts, n_tiles):
    real_refs, meta_refs = refs[:n_rows], refs[n_rows:2 * n_rows]
    const_refs = refs[2 * n_rows:-2]
    out_real, out_meta = refs[-2:]
    load = lambda r: r[0] if len(r.shape) == 3 else r[...]
    t = pl.program_id(0)

    @pl.when(t < n_tiles)
    def _():
        out_real[...] = _residual_layer(mix, n_mix_consts, [load(r) for r in real_refs], const_refs)

    @pl.when(t == n_tiles)
    def _():
        out_meta[...] = _residual_layer(mix, n_mix_consts, [load(r) for r in meta_refs], const_refs)


def _residual_layer_call(mix, real, real_specs, meta, mix_consts, ln0, ffn_consts, ln1, name):
    n_tiles = REAL_ROWS // ROW_TILE
    tile_of = lambda t: jnp.minimum(t, n_tiles - 1)
    consts = [*mix_consts, *ln0, *ffn_consts, *ln1]
    whole = lambda a: pl.BlockSpec(a.shape, lambda t, nd=a.ndim: (0,) * nd)
    body = functools.partial(_residual_layer_body, mix=mix, n_rows=len(real), n_mix_consts=len(mix_consts),
                             n_tiles=n_tiles)
    return pl.pallas_call(
        body,
        grid=(n_tiles + 1,),
        in_specs=[spec(tile_of) for spec in real_specs] + [whole(a) for a in meta] + [_const_spec(a) for a in consts],
        out_specs=[pl.BlockSpec((ROW_TILE, D_MODEL), lambda t: (tile_of(t), 0)),
                   pl.BlockSpec((META_ROWS, D_MODEL), lambda t: (0, 0))],
        out_shape=[jax.ShapeDtypeStruct((REAL_ROWS, D_MODEL), F32), jax.ShapeDtypeStruct((META_ROWS, D_MODEL), F32)],
        compiler_params=_params(1),
        name=name,
    )(*real, *meta, *consts)


def _flat_spec(tile_of):
    return pl.BlockSpec((ROW_TILE, D_MODEL), lambda t: (tile_of(t), 0))


def _batch_major_spec(tile_of):
    return pl.BlockSpec((1, ROW_TILE, D_MODEL), lambda t: (tile_of(t) % BATCH, tile_of(t) // BATCH, 0))


def _rms_rope(t, gain, cos, sin, ones_blk):
    lane = lax.broadcasted_iota(jnp.int32, (t.shape[0], LANES), 1)
    first_half = (lane % ROPE_AXIS_DIM) < (ROPE_AXIS_DIM // 2)
    cols = []
    for c in range(t.shape[1] // LANES):
        tc = t[:, c * LANES:(c + 1) * LANES]
        sq = tc * tc
        hi = sq.astype(BF16)
        lo = (sq - hi.astype(F32)).astype(BF16)
        ms = _dot(hi, ones_blk) + _dot(lo, ones_blk)
        tn = tc * lax.rsqrt(ms + QK_EPS) * gain[:, c * LANES:(c + 1) * LANES]
        half = ROPE_AXIS_DIM // 2
        partner = jnp.where(first_half, pltpu.roll(tn, LANES - half, axis=1), pltpu.roll(tn, half, axis=1))
        cols.append(tn * cos + partner * sin)
    return jnp.concatenate(cols, axis=1)


def _qkv_body(h_ref, w_ref, qg_ref, kg_ref, cos_ref, sin_ref, ones_ref, q_ref, k_ref, v_ref, *, v_transposed):
    qkv = _dot(h_ref[...].astype(BF16), w_ref[...])
    cos, sin, ones_blk = cos_ref[...], sin_ref[...], ones_ref[...]
    q = _rms_rope(qkv[:, :Q_WIDTH], qg_ref[...], cos, sin, ones_blk)
    k = _rms_rope(qkv[:, Q_WIDTH:Q_WIDTH + KV_WIDTH], kg_ref[...], cos, sin, ones_blk)
    v = qkv[:, Q_WIDTH + KV_WIDTH:]
    q_ref[0] = (q * (HEAD_DIM ** -0.5 * LOG2_E)).astype(BF16)
    if v_transposed:
        vt = v.T
    for g in range(N_KV_HEADS):
        k_ref[0, g] = k[:, g * HEAD_DIM:(g + 1) * HEAD_DIM].astype(BF16)
        if v_transposed:
            v_ref[0, g, 0:HEAD_DIM, :] = vt[g * HEAD_DIM:(g + 1) * HEAD_DIM, :].astype(BF16)
            v_ref[0, g, HEAD_DIM:, :] = jnp.ones((V_ROWS - HEAD_DIM, vt.shape[1]), BF16)
        else:
            v_ref[0, g] = v[:, g * HEAD_DIM:(g + 1) * HEAD_DIM].astype(BF16)


def _qkv(h, w, qg, kg, cos, sin, ones_blk, tile, name, v_transposed):
    rows = h.shape[0]
    per_batch = rows // BATCH
    consts_a = [w, qg, kg]
    if v_transposed:
        v_spec = pl.BlockSpec((1, N_KV_HEADS, V_ROWS, tile), lambda t: (t % BATCH, 0, 0, t // BATCH))
        v_shape = jax.ShapeDtypeStruct((BATCH, N_KV_HEADS, V_ROWS, per_batch), BF16)
    else:
        v_spec = pl.BlockSpec((1, N_KV_HEADS, tile, HEAD_DIM), lambda t: (t % BATCH, 0, t // BATCH, 0))
        v_shape = jax.ShapeDtypeStruct((BATCH, N_KV_HEADS, per_batch, HEAD_DIM), BF16)
    return pl.pallas_call(
        functools.partial(_qkv_body, v_transposed=v_transposed),
        grid=(rows // tile,),
        in_specs=[pl.BlockSpec((tile, D_MODEL), lambda t: (t, 0))] + [_const_spec(a) for a in consts_a]
        + [pl.BlockSpec((tile, LANES), lambda t: (t, 0)), pl.BlockSpec((tile, LANES), lambda t: (t, 0)),
           _const_spec(ones_blk)],
        out_specs=[pl.BlockSpec((1, tile, Q_WIDTH), lambda t: (t % BATCH, t // BATCH, 0)),
                   pl.BlockSpec((1, N_KV_HEADS, tile, HEAD_DIM), lambda t: (t % BATCH, 0, t // BATCH, 0)),
                   v_spec],
        out_shape=[jax.ShapeDtypeStruct((BATCH, per_batch, Q_WIDTH), BF16),
                   jax.ShapeDtypeStruct((BATCH, N_KV_HEADS, per_batch, HEAD_DIM), BF16),
                   v_shape],
        compiler_params=_params(1),
        name=name,
    )(h, *consts_a, cos, sin, ones_blk)


def _flash_body(q_ref, k_ref, vt_ref, km_ref, vtm_ref, o_ref, qs_ref, m_ref, acc_ref, sa_ref, sb_ref, *,
                tq, n_key_tiles):
    rows = Q_PER_KV * tq
    n_col_blocks = 2 if rows % (2 * LANES) == 0 else 1
    cw = rows // n_col_blocks
    qb = q_ref[0]
    for h in range(Q_PER_KV):
        qs_ref[h * tq:(h + 1) * tq, :] = qb[:, h * HEAD_DIM:(h + 1) * HEAD_DIM]

    def scores(k, c):
        return lax.dot_general(k, qs_ref[c * cw:(c + 1) * cw, :], (((1,), (1,)), ((), ())),
                               preferred_element_type=F32)

    def absorb(st, vt, c):
        cols = slice(c * cw, (c + 1) * cw)
        m_prev = m_ref[:, cols]
        m_new = jnp.maximum(m_prev, jnp.max(st, axis=0, keepdims=True))
        alpha = jnp.exp2(m_prev - m_new)
        pt = jnp.exp2(st - m_new).astype(BF16)
        acc_ref[:, cols] = alpha * acc_ref[:, cols] + _dot(vt, pt)
        m_ref[:, cols] = m_new

    def k_tile(j):
        return k_ref[0, 0, pl.ds(pl.multiple_of(j * KEY_TILE, KEY_TILE), KEY_TILE), :]

    def v_tile(j):
        return vt_ref[0, 0, :, pl.ds(pl.multiple_of(j * KEY_TILE, KEY_TILE), KEY_TILE)]

    def step(j, cur_ref, next_ref):
        k_next = None if next_ref is None else k_tile(j + 1)
        vt = v_tile(j)
        for c in range(n_col_blocks):
            cols = slice(c * cw, (c + 1) * cw)
            if next_ref is not None:
                next_ref[:, cols] = scores(k_next, c)
            absorb(cur_ref[:, cols], vt, c)

    stm = lax.dot_general(km_ref[0, 0], qs_ref[...], (((1,), (1,)), ((), ())), preferred_element_type=F32)
    m_meta = jnp.max(stm, axis=0, keepdims=True)
    m_ref[...] = m_meta
    acc_ref[...] = _dot(vtm_ref[0, 0], jnp.exp2(stm - m_meta).astype(BF16))
    k_first = k_tile(0)
    for c in range(n_col_blocks):
        sa_ref[:, c * cw:(c + 1) * cw] = scores(k_first, c)

    def pair_step(jj, carry):
        step(2 * jj, sa_ref, sb_ref)
        step(2 * jj + 1, sb_ref, sa_ref)
        return carry

    lax.fori_loop(0, n_key_tiles // 2 - 1, pair_step, 0)
    step(n_key_tiles - 2, sa_ref, sb_ref)
    step(n_key_tiles - 1, sb_ref, None)
    acc = acc_ref[...]
    out = (acc[:HEAD_DIM, :] / acc[HEAD_DIM:HEAD_DIM + 1, :]).T
    o_ref[0] = jnp.concatenate([out[h * tq:(h + 1) * tq, :] for h in range(Q_PER_KV)], axis=1).astype(BF16)


def _flash(q, k, vt, km, vtm, tq, name):
    lq = q.shape[1]
    rows = Q_PER_KV * tq
    head = lambda b, g, t: (b, g, 0, 0)
    return pl.pallas_call(
        functools.partial(_flash_body, tq=tq, n_key_tiles=SEQ // KEY_TILE),
        grid=(BATCH, N_KV_HEADS, lq // tq),
        in_specs=[pl.BlockSpec((1, tq, Q_PER_KV * HEAD_DIM), lambda b, g, t: (b, t, g)),
                  pl.BlockSpec((1, 1, SEQ, HEAD_DIM), head), pl.BlockSpec((1, 1, V_ROWS, SEQ), head),
                  pl.BlockSpec((1, 1, N_META, HEAD_DIM), head), pl.BlockSpec((1, 1, V_ROWS, N_META), head)],
        out_specs=pl.BlockSpec((1, tq, Q_PER_KV * HEAD_DIM), lambda b, g, t: (b, t, g)),
        out_shape=jax.ShapeDtypeStruct((BATCH, lq, Q_WIDTH), BF16),
        scratch_shapes=[pltpu.VMEM((rows, HEAD_DIM), BF16), pltpu.VMEM((1, rows), F32),
                        pltpu.VMEM((V_ROWS, rows), F32),
                        pltpu.VMEM((KEY_TILE, rows), F32), pltpu.VMEM((KEY_TILE, rows), F32)],
        compiler_params=_params(3),
        name=name,
    )(q, k, vt, km, vtm)


def _complex_axpy(ar, ai, sr, si, br, bi):
    return ar * sr - ai * si + br, ar * si + ai * sr + bi


def _s5_sweep(bu_ref, a_ref, order, lc, sr, si, emit):
    ar = a_ref[0, :, lc:lc + SCAN_LANES]
    ai = a_ref[1, :, lc:lc + SCAN_LANES]
    for i in order:
        bur = bu_ref[0, i * SCAN_ROWS:(i + 1) * SCAN_ROWS, lc:lc + SCAN_LANES]
        bui = bu_ref[1, i * SCAN_ROWS:(i + 1) * SCAN_ROWS, lc:lc + SCAN_LANES]
        sr, si = _complex_axpy(ar, ai, sr, si, bur, bui)
        emit(i, sr, si)
    return sr, si


def _s5_dir_body(x_ref, init_ref, wb_ref, wc_ref, a_ref, a16_ref, y_ref, end_ref,
                 bu_ref, st_ref, loc_ref, sin_ref, carry_ref, *, reverse, chain_rows):
    b, j = pl.program_id(0), pl.program_id(1)
    order = range(CHUNK - 1, -1, -1) if reverse else range(CHUNK)
    if chain_rows:
        @pl.when(j == 0)
        def _():
            carry_ref[0:1, :] = init_ref[pl.ds(b, 1), :]

    xb = x_ref[...].reshape(CHUNK * SCAN_ROWS, D_MODEL).astype(BF16)
    for ct in range(N_CH_TILES):
        re = slice(ct * STATE_TILE, (ct + 1) * STATE_TILE)
        im = slice(STATE_W + ct * STATE_TILE, STATE_W + (ct + 1) * STATE_TILE)
        xct = xb[:, ct * MXU_TILE:(ct + 1) * MXU_TILE]
        bu_ref[0, :, re] = _dot(xct, wb_ref[0, ct])
        bu_ref[1, :, re] = _dot(xct, wb_ref[1, ct])

        for lc in range(re.start, re.stop, SCAN_LANES):
            zero = jnp.zeros((SCAN_ROWS, SCAN_LANES), F32)
            sr, si = _s5_sweep(bu_ref, a_ref, order, lc, zero, zero, lambda i, sr, si: None)
            loc_ref[:, lc:lc + SCAN_LANES] = sr
            loc_ref[:, STATE_W + lc:STATE_W + lc + SCAN_LANES] = si

        pr, pi = a16_ref[0, :, re], a16_ref[1, :, re]
        if chain_rows:
            cr, ci = carry_ref[0:1, re], carry_ref[0:1, im]
            for r in order:
                sin_ref[r:r + 1, re] = cr
                sin_ref[r:r + 1, im] = ci
                cr, ci = _complex_axpy(pr, pi, cr, ci, loc_ref[r:r + 1, re], loc_ref[r:r + 1, im])
            carry_ref[0:1, re] = cr
            carry_ref[0:1, im] = ci
            end_ref[0, :, re] = jnp.broadcast_to(cr, (end_ref.shape[1], STATE_TILE))
            end_ref[0, :, im] = jnp.broadcast_to(ci, (end_ref.shape[1], STATE_TILE))
        else:
            sr, si = init_ref[:, re], init_ref[:, im]
            sin_ref[:, re] = sr
            sin_ref[:, im] = si
            er, ei = _complex_axpy(pr, pi, sr, si, loc_ref[:, re], loc_ref[:, im])
            end_ref[:, re] = er
            end_ref[:, im] = ei

        for lc in range(re.start, re.stop, SCAN_LANES):
            def emit(i, sr, si, lc=lc):
                st_ref[0, i * SCAN_ROWS:(i + 1) * SCAN_ROWS, lc:lc + SCAN_LANES] = sr.astype(BF16)
                st_ref[1, i * SCAN_ROWS:(i + 1) * SCAN_ROWS, lc:lc + SCAN_LANES] = si.astype(BF16)
            _s5_sweep(bu_ref, a_ref, order, lc, sin_ref[:, lc:lc + SCAN_LANES],
                      sin_ref[:, STATE_W + lc:STATE_W + lc + SCAN_LANES], emit)

        yct = _dot(st_ref[0, :, re], wc_ref[0, ct]) + _dot(st_ref[1, :, re], wc_ref[1, ct])
        y_ref[:, :, ct * MXU_TILE:(ct + 1) * MXU_TILE] = yct.reshape(CHUNK, SCAN_ROWS, MXU_TILE)


def _s5_dir(x3, init, wb, wc, a, a16, reverse, chain_rows, name):
    n_rows = x3.shape[1]
    if chain_rows:
        n_batch, n_blocks = BATCH, n_rows // (BATCH * SCAN_ROWS)
        end_shape, end_spec = (BATCH, 8, 2 * STATE_W), pl.BlockSpec((1, 8, 2 * STATE_W), lambda b, j: (b, 0, 0))
    else:
        n_batch, n_blocks = 1, n_rows // SCAN_ROWS
        end_shape, end_spec = init.shape, pl.BlockSpec(init.shape, lambda b, j: (0, 0))

    def row_block(b, j):
        return (0, b * n_blocks + (n_blocks - 1 - j if reverse else j), 0)

    consts = [init, wb, wc, a, a16]
    return pl.pallas_call(
        functools.partial(_s5_dir_body, reverse=reverse, chain_rows=chain_rows),
        grid=(n_batch, n_blocks),
        in_specs=[pl.BlockSpec((CHUNK, SCAN_ROWS, D_MODEL), row_block)] + [_const_spec(c) for c in consts],
        out_specs=[pl.BlockSpec((CHUNK, SCAN_ROWS, D_MODEL), row_block), end_spec],
        out_shape=[jax.ShapeDtypeStruct(x3.shape, F32), jax.ShapeDtypeStruct(end_shape, F32)],
        scratch_shapes=[pltpu.VMEM((2, CHUNK * SCAN_ROWS, STATE_W), F32),
                        pltpu.VMEM((2, CHUNK * SCAN_ROWS, STATE_W), BF16),
                        pltpu.VMEM((SCAN_ROWS, 2 * STATE_W), F32),
                        pltpu.VMEM((SCAN_ROWS, 2 * STATE_W), F32),
                        pltpu.VMEM((8, 2 * STATE_W), F32)],
        compiler_params=_params(2),
        name=name,
    )(x3, *consts)


def _s5_weights(lam_re, lam_im, log_dt, b_re, b_im, c_re, c_im):
    dt = jnp.exp(log_dt)[..., None]
    mag = jnp.exp(lam_re * dt)
    abr = mag * jnp.cos(lam_im * dt)
    abi = mag * jnp.sin(lam_im * dt)
    nr, ni = abr - 1.0, abi
    den = lam_re * lam_re + lam_im * lam_im
    cr = (nr * lam_re + ni * lam_im) / den
    ci = (ni * lam_re - nr * lam_im) / den
    bbr = cr[..., None] * b_re - ci[..., None] * b_im
    bbi = cr[..., None] * b_im + ci[..., None] * b_re
    ch_group = jnp.arange(MXU_TILE) // S5_GROUP_CH
    st_group = jnp.arange(STATE_TILE) // S5_STATE

    def blockdiag_in(m):
        m = m.reshape(2, N_CH_TILES, GROUPS_PER_TILE, S5_STATE, S5_GROUP_CH)
        row = m.transpose(0, 1, 4, 2, 3).reshape(2, N_CH_TILES, 1, S5_GROUP_CH, STATE_TILE)
        full = jnp.broadcast_to(row, (2, N_CH_TILES, GROUPS_PER_TILE, S5_GROUP_CH, STATE_TILE))
        full = full.reshape(2, N_CH_TILES, MXU_TILE, STATE_TILE)
        return jnp.where(ch_group[:, None] == st_group[None, :], full, 0.0)

    def blockdiag_out(m):
        m = m.reshape(2, N_CH_TILES, GROUPS_PER_TILE, S5_GROUP_CH, S5_STATE)
        col = m.transpose(0, 1, 2, 4, 3).reshape(2, N_CH_TILES, STATE_TILE, 1, S5_GROUP_CH)
        full = jnp.broadcast_to(col, (2, N_CH_TILES, STATE_TILE, GROUPS_PER_TILE, S5_GROUP_CH))
        full = full.reshape(2, N_CH_TILES, STATE_TILE, MXU_TILE)
        return jnp.where(st_group[:, None] == ch_group[None, :], full, 0.0)

    wb = jnp.stack([blockdiag_in(bbr), blockdiag_in(bbi)], axis=1).astype(BF16)
    wc = jnp.stack([blockdiag_out(c_re), blockdiag_out(-c_im)], axis=1).astype(BF16)
    a = jnp.stack([abr, abi], axis=1).reshape(2, 2, 1, STATE_W)
    pr, pi = abr, abi
    for _ in range(int(math.log2(CHUNK))):
        pr, pi = pr * pr - pi * pi, 2.0 * pr * pi
    a16 = jnp.stack([pr, pi], axis=1).reshape(2, 2, 1, STATE_W)
    return wb, wc, a, a16


def _s5_scan(hr, hm, wb, wc, a, a16):
    x3 = hr.reshape(CHUNK, CHUNK_ROWS, D_MODEL)
    xm3 = jnp.zeros((CHUNK, SCAN_ROWS, D_MODEL), F32).at[:, :BATCH].set(
        hm.reshape(BATCH, N_META, D_MODEL).transpose(1, 0, 2))
    zero_state = jnp.zeros((SCAN_ROWS, 2 * STATE_W), F32)
    fwd = lambda x, init, chain, name: _s5_dir(x, init, wb[0], wc[0], a[0], a16[0], False, chain, name)
    rev = lambda x, init, chain, name: _s5_dir(x, init, wb[1], wc[1], a[1], a16[1], True, chain, name)
    ymf3, meta_end = fwd(xm3, zero_state, False, "s5_fwd_meta")
    yf3, _ = fwd(x3, meta_end, True, "s5_fwd_real")
    yr3, real_end = rev(x3, zero_state, True, "s5_rev_real")
    ymr3, _ = rev(xm3, zero_state.at[:BATCH].set(real_end[:, 0]), False, "s5_rev_meta")
    flat_meta = lambda y3: y3[:, :BATCH].transpose(1, 0, 2).reshape(META_ROWS, D_MODEL)
    return ((yf3.reshape(REAL_ROWS, D_MODEL), yr3.reshape(REAL_ROWS, D_MODEL)),
            (flat_meta(ymf3), flat_meta(ymr3)))


def _rope_tables():
    slot = jnp.arange(CHUNK, dtype=jnp.int32)[:, None, None]
    chunk = jnp.arange(N_CHUNKS, dtype=jnp.int32)[None, None, :]
    tok = jnp.broadcast_to(chunk * CHUNK + slot, (CHUNK, BATCH, N_CHUNKS)).reshape(-1)
    row_id = (tok // GRID_W).astype(F32)
    col_id = (tok % GRID_W).astype(F32)
    inv_freq = ROPE_THETA ** (-jnp.arange(0, ROPE_AXIS_DIM, 2, dtype=F32) / ROPE_AXIS_DIM)
    ang_r = row_id[:, None] * inv_freq[None, :]
    ang_c = col_id[:, None] * inv_freq[None, :]
    cos_r, sin_r, cos_c, sin_c = jnp.cos(ang_r), jnp.sin(ang_r), jnp.cos(ang_c), jnp.sin(ang_c)
    cos_head = jnp.concatenate([cos_r, cos_r, cos_c, cos_c], axis=1)
    sin_head = jnp.concatenate([-sin_r, sin_r, -sin_c, sin_c], axis=1)
    reps = LANES // HEAD_DIM
    cos, sin = jnp.tile(cos_head, (1, reps)), jnp.tile(sin_head, (1, reps))
    cos_meta = jnp.ones((META_ROWS, LANES), F32)
    sin_meta = jnp.zeros((META_ROWS, LANES), F32)
    return cos, sin, cos_meta, sin_meta


def kernel(x, meta_tokens, s5_lambda_re, s5_lambda_im, s5_log_dt, s5_b_re, s5_b_im, s5_c_re, s5_c_im, s5_d,
           s5_w_glu, s5_w_out, attn_w_qkv, attn_q_gain, attn_k_gain, attn_w_out, ffn_w_gate, ffn_w_up,
           ffn_w_down, ln_gain, ln_bias):
    hr = x.reshape(BATCH, N_CHUNKS, CHUNK, D_MODEL).transpose(2, 0, 1, 3).reshape(REAL_ROWS, D_MODEL)
    hm = jnp.broadcast_to(meta_tokens[None], (BATCH, N_META, D_MODEL)).reshape(META_ROWS, D_MODEL)
    cos, sin, cos_meta, sin_meta = _rope_tables()
    head_id = jnp.arange(LANES) // HEAD_DIM
    ones_blk = ((head_id[:, None] == head_id[None, :]).astype(F32) / HEAD_DIM).astype(BF16)

    for i in range(DEPTH):
        j = i // 2
        ln0 = [ln_gain[i, 0][None], ln_bias[i, 0][None]]
        ln1 = [ln_gain[i, 1][None], ln_bias[i, 1][None]]
        ffn_w = [ffn_w_gate[i].astype(BF16), ffn_w_up[i].astype(BF16), ffn_w_down[i].astype(BF16)]
        if i % 2 == 0:
            wb, wc, a, a16 = _s5_weights(s5_lambda_re[j], s5_lambda_im[j], s5_log_dt[j], s5_b_re[j], s5_b_im[j],
                                         s5_c_re[j], s5_c_im[j])
            (yf, yr), (ymf, ymr) = _s5_scan(hr, hm, wb, wc, a, a16)
            mix_w = [s5_d[j][None], s5_w_glu[j].astype(BF16), s5_w_out[j].astype(BF16)]
            hr, hm = _residual_layer_call(_s5_mix, [hr, yf, yr], [_flat_spec] * 3, [hm, ymf, ymr],
                                          mix_w, ln0, ffn_w, ln1, "s5_tail_ffn")
        else:
            wqkv = attn_w_qkv[j].astype(BF16)
            qg = jnp.tile(attn_q_gain[j], N_Q_HEADS)[None]
            kg = jnp.tile(attn_k_gain[j], N_KV_HEADS)[None]
            q, k, vt = _qkv(hr, wqkv, qg, kg, cos, sin, ones_blk, ROW_TILE, "qkv_real", True)
            qm, km, vm = _qkv(hm, wqkv, qg, kg, cos_meta, sin_meta, ones_blk, N_META, "qkv_meta", False)
            vtm = jnp.concatenate([vm.swapaxes(2, 3), jnp.ones((BATCH, N_KV_HEADS, V_ROWS - HEAD_DIM, N_META), BF16)],
                                  axis=2)
            o = _flash(q, k, vt, km, vtm, Q_TILE, "flash_real")
            om = _flash(qm, k, vt, km, vtm, N_META, "flash_meta").reshape(META_ROWS, D_MODEL)
            hr, hm = _residual_layer_call(_attn_mix, [hr, o], [_flat_spec, _batch_major_spec], [hm, om],
                                          [attn_w_out[j].astype(BF16)], ln0, ffn_w, ln1, "attn_tail_ffn")
    return hr.reshape(CHUNK, BATCH, N_CHUNKS, D_MODEL).transpose(1, 2, 0, 3).reshape(BATCH, SEQ, D_MODEL)
```

```python
import functools
import math

import jax
import jax.numpy as jnp
from jax import lax
from jax.experimental import pallas as pl
from jax.experimental.pallas import tpu as pltpu

D_MODEL = 1024
BATCH = 2
SEQ = 8192
DEPTH = 4
N_META = 16
GRID_W = 64
HEAD_DIM = 64
N_Q_HEADS = D_MODEL // HEAD_DIM
N_KV_HEADS = N_Q_HEADS // 4
Q_PER_KV = N_Q_HEADS // N_KV_HEADS
Q_WIDTH = N_Q_HEADS * HEAD_DIM
KV_WIDTH = N_KV_HEADS * HEAD_DIM
ROPE_THETA = 10000.0
ROPE_AXIS_DIM = HEAD_DIM // 2
QK_EPS = 1e-6
S5_GROUP_CH = 16
S5_GROUPS = D_MODEL // S5_GROUP_CH
S5_STATE = 64
D_FF = -(-8 * D_MODEL // (3 * 256)) * 256
LN_EPS = 1e-5
DEEPNORM_ALPHA = (2.0 * DEPTH) ** 0.25

LANES = 128
MXU_TILE = 256
CHUNK = 16
N_CHUNKS = SEQ // CHUNK
REAL_ROWS = BATCH * SEQ
CHUNK_ROWS = BATCH * N_CHUNKS
META_ROWS = BATCH * N_META
ROW_TILE = 512
SCAN_ROWS = 16
STATE_W = S5_GROUPS * S5_STATE
GROUPS_PER_TILE = MXU_TILE // S5_GROUP_CH
STATE_TILE = GROUPS_PER_TILE * S5_STATE
N_CH_TILES = D_MODEL // MXU_TILE
SCAN_LANES = 256
Q_TILE = 256
KEY_TILE = 1024
V_ROWS = 2 * HEAD_DIM
LOG2_E = math.log2(math.e)
VMEM_LIMIT = 56 * 1024 * 1024

F32 = jnp.float32
BF16 = jnp.bfloat16


def _const_spec(a):
    return pl.BlockSpec(a.shape, lambda *_, nd=a.ndim: (0,) * nd, pipeline_mode=pl.Buffered(1))


def _params(n_axes):
    return pltpu.CompilerParams(dimension_semantics=("arbitrary",) * n_axes, vmem_limit_bytes=VMEM_LIMIT)


def _dot(a, b):
    return jnp.dot(a, b, preferred_element_type=F32)


def _layer_norm(z, gain, bias):
    mean = jnp.mean(z, axis=-1, keepdims=True)
    zc = z - mean
    var = jnp.mean(zc * zc, axis=-1, keepdims=True)
    return zc * lax.rsqrt(var + LN_EPS) * gain + bias


def _s5_mix(h, yf, yr, d_ref, wglu_ref, wout_ref):
    y = yf + yr + d_ref[...] * h
    g = 0.5 * y * (1.0 + lax.erf(y * math.sqrt(0.5)))
    zz = (g * jax.nn.sigmoid(_dot(g.astype(BF16), wglu_ref[...]))).astype(BF16)
    return _dot(zz, wout_ref[...])


def _attn_mix(h, o, wout_ref):
    return _dot(o, wout_ref[...])


def _residual_layer(mix, n_mix_consts, vals, const_refs):
    mix_refs = const_refs[:n_mix_consts]
    gain0, bias0, wg, wu, wd, gain1, bias1 = const_refs[n_mix_consts:]
    h = vals[0]
    h = _layer_norm(DEEPNORM_ALPHA * h + mix(*vals, *mix_refs), gain0[...], bias0[...])
    hb = h.astype(BF16)
    gate = _dot(hb, wg[...])
    up = _dot(hb, wu[...])
    act = (gate * jax.nn.sigmoid(gate) * up).astype(BF16)
    return _layer_norm(DEEPNORM_ALPHA * h + _dot(act, wd[...]), gain1[...], bias1[...])


def _residual_layer_body(*refs, mix, n_rows, n_mix_consts, n_tiles):
    real_refs, meta_refs = refs[:n_rows], refs[n_rows:2 * n_rows]
    const_refs = refs[2 * n_rows:-2]
    out_real, out_meta = refs[-2:]
    load = lambda r: r[0] if len(r.shape) == 3 else r[...]
    t = pl.program_id(0)

    @pl.when(t < n_tiles)
    def _():
        out_real[...] = _residual_layer(mix, n_mix_consts, [load(r) for r in real_refs], const_refs)

    @pl.when(t == n_tiles)
    def _():
        out_meta[...] = _residual_layer(mix, n_mix_consts, [load(r) for r in meta_refs], const_refs)


def _residual_layer_call(mix, real, real_specs, meta, mix_consts, ln0, ffn_consts, ln1, name):
    n_tiles = REAL_ROWS // ROW_TILE
    tile_of = lambda t: jnp.minimum(t, n_tiles - 1)
    consts = [*mix_consts, *ln0, *ffn_consts, *ln1]
    whole = lambda a: pl.BlockSpec(a.shape, lambda t, nd=a.ndim: (0,) * nd)
    body = functools.partial(_residual_layer_body, mix=mix, n_rows=len(real), n_mix_consts=len(mix_consts),
                             n_tiles=n_tiles)
    return pl.pallas_call(
        body,
        grid=(n_tiles + 1,),
        in_specs=[spec(tile_of) for spec in real_specs] + [whole(a) for a in meta] + [_const_spec(a) for a in consts],
        out_specs=[pl.BlockSpec((ROW_TILE, D_MODEL), lambda t: (tile_of(t), 0)),
                   pl.BlockSpec((META_ROWS, D_MODEL), lambda t: (0, 0))],
        out_shape=[jax.ShapeDtypeStruct((REAL_ROWS, D_MODEL), F32), jax.ShapeDtypeStruct((META_ROWS, D_MODEL), F32)],
        compiler_params=_params(1),
        name=name,
    )(*real, *meta, *consts)


def _flat_spec(tile_of):
    return pl.BlockSpec((ROW_TILE, D_MODEL), lambda t: (tile_of(t), 0))


def _batch_major_spec(tile_of):
    return pl.BlockSpec((1, ROW_TILE, D_MODEL), lambda t: (tile_of(t) % BATCH, tile_of(t) // BATCH, 0))


def _rms_rope(t, gain, cos, sin, ones_blk):
    lane = lax.broadcasted_iota(jnp.int32, (t.shape[0], LANES), 1)
    first_half = (lane % ROPE_AXIS_DIM) < (ROPE_AXIS_DIM // 2)
    cols = []
    for c in range(t.shape[1] // LANES):
        tc = t[:, c * LANES:(c + 1) * LANES]
        sq = tc * tc
        hi = sq.astype(BF16)
        lo = (sq - hi.astype(F32)).astype(BF16)
        ms = _dot(hi, ones_blk) + _dot(lo, ones_blk)
        tn = tc * lax.rsqrt(ms + QK_EPS) * gain[:, c * LANES:(c + 1) * LANES]
        half = ROPE_AXIS_DIM // 2
        partner = jnp.where(first_half, pltpu.roll(tn, LANES - half, axis=1), pltpu.roll(tn, half, axis=1))
        cols.append(tn * cos + partner * sin)
    return jnp.concatenate(cols, axis=1)


def _qkv_body(h_ref, w_ref, qg_ref, kg_ref, cos_ref, sin_ref, ones_ref, q_ref, k_ref, v_ref, *, v_transposed):
    qkv = _dot(h_ref[...].astype(BF16), w_ref[...])
    cos, sin, ones_blk = cos_ref[...], sin_ref[...], ones_ref[...]
    q = _rms_rope(qkv[:, :Q_WIDTH], qg_ref[...], cos, sin, ones_blk)
    k = _rms_rope(qkv[:, Q_WIDTH:Q_WIDTH + KV_WIDTH], kg_ref[...], cos, sin, ones_blk)
    v = qkv[:, Q_WIDTH + KV_WIDTH:]
    q_ref[0] = (q * (HEAD_DIM ** -0.5 * LOG2_E)).astype(BF16)
    if v_transposed:
        vt = v.T
    for g in range(N_KV_HEADS):
        k_ref[0, g] = k[:, g * HEAD_DIM:(g + 1) * HEAD_DIM].astype(BF16)
        if v_transposed:
            v_ref[0, g, 0:HEAD_DIM, :] = vt[g * HEAD_DIM:(g + 1) * HEAD_DIM, :].astype(BF16)
            v_ref[0, g, HEAD_DIM:, :] = jnp.ones((V_ROWS - HEAD_DIM, vt.shape[1]), BF16)
        else:
            v_ref[0, g] = v[:, g * HEAD_DIM:(g + 1) * HEAD_DIM].astype(BF16)


def _qkv(h, w, qg, kg, cos, sin, ones_blk, tile, name, v_transposed):
    rows = h.shape[0]
    per_batch = rows // BATCH
    consts_a = [w, qg, kg]
    if v_transposed:
        v_spec = pl.BlockSpec((1, N_KV_HEADS, V_ROWS, tile), lambda t: (t % BATCH, 0, 0, t // BATCH))
        v_shape = jax.ShapeDtypeStruct((BATCH, N_KV_HEADS, V_ROWS, per_batch), BF16)
    else:
        v_spec = pl.BlockSpec((1, N_KV_HEADS, tile, HEAD_DIM), lambda t: (t % BATCH, 0, t // BATCH, 0))
        v_shape = jax.ShapeDtypeStruct((BATCH, N_KV_HEADS, per_batch, HEAD_DIM), BF16)
    return pl.pallas_call(
        functools.partial(_qkv_body, v_transposed=v_transposed),
        grid=(rows // tile,),
        in_specs=[pl.BlockSpec((tile, D_MODEL), lambda t: (t, 0))] + [_const_spec(a) for a in consts_a]
        + [pl.BlockSpec((tile, LANES), lambda t: (t, 0)), pl.BlockSpec((tile, LANES), lambda t: (t, 0)),
           _const_spec(ones_blk)],
        out_specs=[pl.BlockSpec((1, tile, Q_WIDTH), lambda t: (t % BATCH, t // BATCH, 0)),
                   pl.BlockSpec((1, N_KV_HEADS, tile, HEAD_DIM), lambda t: (t % BATCH, 0, t // BATCH, 0)),
                   v_spec],
        out_shape=[jax.ShapeDtypeStruct((BATCH, per_batch, Q_WIDTH), BF16),
                   jax.ShapeDtypeStruct((BATCH, N_KV_HEADS, per_batch, HEAD_DIM), BF16),
                   v_shape],
        compiler_params=_params(1),
        name=name,
    )(h, *consts_a, cos, sin, ones_blk)


def _flash_body(q_ref, qn_ref, k_ref, vt_ref, km_ref, vtm_ref, o_ref, qs_ref, qsn_ref, m_ref, acc_ref, sa_ref,
                sb_ref, *, tq, n_key_tiles):
    rows = Q_PER_KV * tq
    n_col_blocks = 2 if rows % (2 * LANES) == 0 else 1
    cw = rows // n_col_blocks

    def stack_heads(src_ref, dst_ref):
        qb = src_ref[0]
        for h in range(Q_PER_KV):
            dst_ref[h * tq:(h + 1) * tq, :] = qb[:, h * HEAD_DIM:(h + 1) * HEAD_DIM]

    def scores(k, c, stacked_ref=qs_ref):
        return lax.dot_general(k, stacked_ref[c * cw:(c + 1) * cw, :], (((1,), (1,)), ((), ())),
                               preferred_element_type=F32)

    @pl.when(pl.program_id(2) == 0)
    def _():
        stack_heads(q_ref, qs_ref)
        k_first = k_ref[0, 0, 0:KEY_TILE, :]
        for c in range(n_col_blocks):
            sa_ref[:, c * cw:(c + 1) * cw] = scores(k_first, c)

    @pl.when(pl.program_id(2) > 0)
    def _():
        qs_ref[...] = qsn_ref[...]

    stack_heads(qn_ref, qsn_ref)

    def absorb(st, vt, c):
        cols = slice(c * cw, (c + 1) * cw)
        m_prev = m_ref[:, cols]
        m_new = jnp.maximum(m_prev, jnp.max(st, axis=0, keepdims=True))
        alpha = jnp.exp2(m_prev - m_new)
        pt = jnp.exp2(st - m_new).astype(BF16)
        acc_ref[:, cols] = alpha * acc_ref[:, cols] + _dot(vt, pt)
        m_ref[:, cols] = m_new

    def key_span(j):
        start = j * KEY_TILE
        return pl.ds(start if isinstance(j, int) else pl.multiple_of(start, KEY_TILE), KEY_TILE)

    def k_tile(j):
        return k_ref[0, 0, key_span(j), :]

    def v_tile(j):
        return vt_ref[0, 0, :, key_span(j)]

    def step(j, cur_ref, next_ref, next_tile, stacked_ref=qs_ref):
        k_next = k_tile(next_tile)
        vt = v_tile(j)
        for c in range(n_col_blocks):
            cols = slice(c * cw, (c + 1) * cw)
            next_ref[:, cols] = scores(k_next, c, stacked_ref)
            absorb(cur_ref[:, cols], vt, c)

    stm = lax.dot_general(km_ref[0, 0], qs_ref[...], (((1,), (1,)), ((), ())), preferred_element_type=F32)
    m_meta = jnp.max(stm, axis=0, keepdims=True)
    m_ref[...] = m_meta
    acc_ref[...] = _dot(vtm_ref[0, 0], jnp.exp2(stm - m_meta).astype(BF16))

    def pair_step(jj, carry):
        step(2 * jj, sa_ref, sb_ref, 2 * jj + 1)
        step(2 * jj + 1, sb_ref, sa_ref, 2 * jj + 2)
        return carry

    lax.fori_loop(0, n_key_tiles // 2 - 1, pair_step, 0)
    step(n_key_tiles - 2, sa_ref, sb_ref, n_key_tiles - 1)
    step(n_key_tiles - 1, sb_ref, sa_ref, 0, qsn_ref)
    acc = acc_ref[...]
    out = (acc[:HEAD_DIM, :] / acc[HEAD_DIM:HEAD_DIM + 1, :]).T
    o_ref[0] = jnp.concatenate([out[h * tq:(h + 1) * tq, :] for h in range(Q_PER_KV)], axis=1).astype(BF16)


def _flash(q, k, vt, km, vtm, tq, name):
    lq = q.shape[1]
    rows = Q_PER_KV * tq
    n_q_tiles = lq // tq
    head = lambda b, g, t: (b, g, 0, 0)
    q_width = Q_PER_KV * HEAD_DIM
    return pl.pallas_call(
        functools.partial(_flash_body, tq=tq, n_key_tiles=SEQ // KEY_TILE),
        grid=(BATCH, N_KV_HEADS, n_q_tiles),
        in_specs=[pl.BlockSpec((1, tq, q_width), lambda b, g, t: (b, t, g)),
                  pl.BlockSpec((1, tq, q_width), lambda b, g, t: (b, jnp.minimum(t + 1, n_q_tiles - 1), g)),
                  pl.BlockSpec((1, 1, SEQ, HEAD_DIM), head), pl.BlockSpec((1, 1, V_ROWS, SEQ), head),
                  pl.BlockSpec((1, 1, N_META, HEAD_DIM), head), pl.BlockSpec((1, 1, V_ROWS, N_META), head)],
        out_specs=pl.BlockSpec((1, tq, q_width), lambda b, g, t: (b, t, g)),
        out_shape=jax.ShapeDtypeStruct((BATCH, lq, Q_WIDTH), BF16),
        scratch_shapes=[pltpu.VMEM((rows, HEAD_DIM), BF16), pltpu.VMEM((rows, HEAD_DIM), BF16),
                        pltpu.VMEM((1, rows), F32), pltpu.VMEM((V_ROWS, rows), F32),
                        pltpu.VMEM((KEY_TILE, rows), F32), pltpu.VMEM((KEY_TILE, rows), F32)],
        compiler_params=_params(3),
        name=name,
    )(q, q, k, vt, km, vtm)


def _complex_axpy(ar, ai, sr, si, br, bi):
    return ar * sr - ai * si + br, ar * si + ai * sr + bi


def _s5_sweep(bu_ref, a_ref, lc, sr, si, emit):
    ar = a_ref[0, :, lc:lc + SCAN_LANES]
    ai = a_ref[1, :, lc:lc + SCAN_LANES]
    for k in range(CHUNK):
        bur = bu_ref[0, k * SCAN_ROWS:(k + 1) * SCAN_ROWS, lc:lc + SCAN_LANES]
        bui = bu_ref[1, k * SCAN_ROWS:(k + 1) * SCAN_ROWS, lc:lc + SCAN_LANES]
        sr, si = _complex_axpy(ar, ai, sr, si, bur, bui)
        emit(k, sr, si)
    return sr, si


def _s5_dir_body(x_ref, init_ref, wb_ref, wc_ref, a_ref, a16_ref, y_ref, end_ref,
                 bu_ref, st_ref, loc_ref, sin_ref, carry_ref, *, reverse, chain_rows):
    b, j = pl.program_id(0), pl.program_id(1)
    order = range(CHUNK - 1, -1, -1) if reverse else range(CHUNK)
    if chain_rows:
        @pl.when(j == 0)
        def _():
            carry_ref[0:1, :] = init_ref[pl.ds(b, 1), :]

    xb = jnp.concatenate([x_ref[i] for i in order], axis=0).astype(BF16)
    for ct in range(N_CH_TILES):
        re = slice(ct * STATE_TILE, (ct + 1) * STATE_TILE)
        im = slice(STATE_W + ct * STATE_TILE, STATE_W + (ct + 1) * STATE_TILE)
        xct = xb[:, ct * MXU_TILE:(ct + 1) * MXU_TILE]
        bu_ref[0, :, re] = _dot(xct, wb_ref[0, ct])
        bu_ref[1, :, re] = _dot(xct, wb_ref[1, ct])

        for lc in range(re.start, re.stop, SCAN_LANES):
            zero = jnp.zeros((SCAN_ROWS, SCAN_LANES), F32)
            sr, si = _s5_sweep(bu_ref, a_ref, lc, zero, zero, lambda k, sr, si: None)
            loc_ref[:, lc:lc + SCAN_LANES] = sr
            loc_ref[:, STATE_W + lc:STATE_W + lc + SCAN_LANES] = si

        pr, pi = a16_ref[0, :, re], a16_ref[1, :, re]
        if chain_rows:
            cr, ci = carry_ref[0:1, re], carry_ref[0:1, im]
            for r in order:
                sin_ref[r:r + 1, re] = cr
                sin_ref[r:r + 1, im] = ci
                cr, ci = _complex_axpy(pr, pi, cr, ci, loc_ref[r:r + 1, re], loc_ref[r:r + 1, im])
            carry_ref[0:1, re] = cr
            carry_ref[0:1, im] = ci
            end_ref[0, :, re] = jnp.broadcast_to(cr, (end_ref.shape[1], STATE_TILE))
            end_ref[0, :, im] = jnp.broadcast_to(ci, (end_ref.shape[1], STATE_TILE))
        else:
            sr, si = init_ref[:, re], init_ref[:, im]
            sin_ref[:, re] = sr
            sin_ref[:, im] = si
            er, ei = _complex_axpy(pr, pi, sr, si, loc_ref[:, re], loc_ref[:, im])
            end_ref[:, re] = er
            end_ref[:, im] = ei

        for lc in range(re.start, re.stop, SCAN_LANES):
            def emit(k, sr, si, lc=lc):
                st_ref[0, k * SCAN_ROWS:(k + 1) * SCAN_ROWS, lc:lc + SCAN_LANES] = sr.astype(BF16)
                st_ref[1, k * SCAN_ROWS:(k + 1) * SCAN_ROWS, lc:lc + SCAN_LANES] = si.astype(BF16)
            _s5_sweep(bu_ref, a_ref, lc, sin_ref[:, lc:lc + SCAN_LANES],
                      sin_ref[:, STATE_W + lc:STATE_W + lc + SCAN_LANES], emit)

        yct = _dot(st_ref[0, :, re], wc_ref[0, ct]) + _dot(st_ref[1, :, re], wc_ref[1, ct])
        for k, i in enumerate(order):
            y_ref[i, :, ct * MXU_TILE:(ct + 1) * MXU_TILE] = yct[k * SCAN_ROWS:(k + 1) * SCAN_ROWS, :]


def _s5_dir(x3, init, wb, wc, a, a16, reverse, chain_rows, name):
    n_rows = x3.shape[1]
    if chain_rows:
        n_batch, n_blocks = BATCH, n_rows // (BATCH * SCAN_ROWS)
        end_shape, end_spec = (BATCH, 8, 2 * STATE_W), pl.BlockSpec((1, 8, 2 * STATE_W), lambda b, j: (b, 0, 0))
    else:
        n_batch, n_blocks = 1, n_rows // SCAN_ROWS
        end_shape, end_spec = init.shape, pl.BlockSpec(init.shape, lambda b, j: (0, 0))

    def row_block(b, j):
        return (0, b * n_blocks + (n_blocks - 1 - j if reverse else j), 0)

    consts = [init, wb, wc, a, a16]
    return pl.pallas_call(
        functools.partial(_s5_dir_body, reverse=reverse, chain_rows=chain_rows),
        grid=(n_batch, n_blocks),
        in_specs=[pl.BlockSpec((CHUNK, SCAN_ROWS, D_MODEL), row_block)] + [_const_spec(c) for c in consts],
        out_specs=[pl.BlockSpec((CHUNK, SCAN_ROWS, D_MODEL), row_block), end_spec],
        out_shape=[jax.ShapeDtypeStruct(x3.shape, F32), jax.ShapeDtypeStruct(end_shape, F32)],
        scratch_shapes=[pltpu.VMEM((2, CHUNK * SCAN_ROWS, STATE_W), F32),
                        pltpu.VMEM((2, CHUNK * SCAN_ROWS, STATE_W), BF16),
                        pltpu.VMEM((SCAN_ROWS, 2 * STATE_W), F32),
                        pltpu.VMEM((SCAN_ROWS, 2 * STATE_W), F32),
                        pltpu.VMEM((8, 2 * STATE_W), F32)],
        compiler_params=_params(2),
        name=name,
    )(x3, *consts)


def _s5_weights(lam_re, lam_im, log_dt, b_re, b_im, c_re, c_im):
    dt = jnp.exp(log_dt)[..., None]
    mag = jnp.exp(lam_re * dt)
    abr = mag * jnp.cos(lam_im * dt)
    abi = mag * jnp.sin(lam_im * dt)
    nr, ni = abr - 1.0, abi
    den = lam_re * lam_re + lam_im * lam_im
    cr = (nr * lam_re + ni * lam_im) / den
    ci = (ni * lam_re - nr * lam_im) / den
    bbr = cr[..., None] * b_re - ci[..., None] * b_im
    bbi = cr[..., None] * b_im + ci[..., None] * b_re
    ch_group = jnp.arange(MXU_TILE) // S5_GROUP_CH
    st_group = jnp.arange(STATE_TILE) // S5_STATE

    def blockdiag_in(m):
        m = m.reshape(2, N_CH_TILES, GROUPS_PER_TILE, S5_STATE, S5_GROUP_CH)
        row = m.transpose(0, 1, 4, 2, 3).reshape(2, N_CH_TILES, 1, S5_GROUP_CH, STATE_TILE)
        full = jnp.broadcast_to(row, (2, N_CH_TILES, GROUPS_PER_TILE, S5_GROUP_CH, STATE_TILE))
        full = full.reshape(2, N_CH_TILES, MXU_TILE, STATE_TILE)
        return jnp.where(ch_group[:, None] == st_group[None, :], full, 0.0)

    def blockdiag_out(m):
        m = m.reshape(2, N_CH_TILES, GROUPS_PER_TILE, S5_GROUP_CH, S5_STATE)
        col = m.transpose(0, 1, 2, 4, 3).reshape(2, N_CH_TILES, STATE_TILE, 1, S5_GROUP_CH)
        full = jnp.broadcast_to(col, (2, N_CH_TILES, STATE_TILE, GROUPS_PER_TILE, S5_GROUP_CH))
        full = full.reshape(2, N_CH_TILES, STATE_TILE, MXU_TILE)
        return jnp.where(st_group[:, None] == ch_group[None, :], full, 0.0)

    wb = jnp.stack([blockdiag_in(bbr), blockdiag_in(bbi)], axis=1).astype(BF16)
    wc = jnp.stack([blockdiag_out(c_re), blockdiag_out(-c_im)], axis=1).astype(BF16)
    a = jnp.stack([abr, abi], axis=1).reshape(2, 2, 1, STATE_W)
    pr, pi = abr, abi
    for _ in range(int(math.log2(CHUNK))):
        pr, pi = pr * pr - pi * pi, 2.0 * pr * pi
    a16 = jnp.stack([pr, pi], axis=1).reshape(2, 2, 1, STATE_W)
    return wb, wc, a, a16


def _s5_scan(hr, hm, wb, wc, a, a16):
    x3 = hr.reshape(CHUNK, CHUNK_ROWS, D_MODEL)
    xm3 = jnp.zeros((CHUNK, SCAN_ROWS, D_MODEL), F32).at[:, :BATCH].set(
        hm.reshape(BATCH, N_META, D_MODEL).transpose(1, 0, 2))
    zero_state = jnp.zeros((SCAN_ROWS, 2 * STATE_W), F32)
    fwd = lambda x, init, chain, name: _s5_dir(x, init, wb[0], wc[0], a[0], a16[0], False, chain, name)
    rev = lambda x, init, chain, name: _s5_dir(x, init, wb[1], wc[1], a[1], a16[1], True, chain, name)
    ymf3, meta_end = fwd(xm3, zero_state, False, "s5_fwd_meta")
    yf3, _ = fwd(x3, meta_end, True, "s5_fwd_real")
    yr3, real_end = rev(x3, zero_state, True, "s5_rev_real")
    ymr3, _ = rev(xm3, zero_state.at[:BATCH].set(real_end[:, 0]), False, "s5_rev_meta")
    flat_meta = lambda y3: y3[:, :BATCH].transpose(1, 0, 2).reshape(META_ROWS, D_MODEL)
    return ((yf3.reshape(REAL_ROWS, D_MODEL), yr3.reshape(REAL_ROWS, D_MODEL)),
            (flat_meta(ymf3), flat_meta(ymr3)))


def _rope_tables():
    slot = jnp.arange(CHUNK, dtype=jnp.int32)[:, None, None]
    chunk = jnp.arange(N_CHUNKS, dtype=jnp.int32)[None, None, :]
    tok = jnp.broadcast_to(chunk * CHUNK + slot, (CHUNK, BATCH, N_CHUNKS)).reshape(-1)
    row_id = (tok // GRID_W).astype(F32)
    col_id = (tok % GRID_W).astype(F32)
    inv_freq = ROPE_THETA ** (-jnp.arange(0, ROPE_AXIS_DIM, 2, dtype=F32) / ROPE_AXIS_DIM)
    ang_r = row_id[:, None] * inv_freq[None, :]
    ang_c = col_id[:, None] * inv_freq[None, :]
    cos_r, sin_r, cos_c, sin_c = jnp.cos(ang_r), jnp.sin(ang_r), jnp.cos(ang_c), jnp.sin(ang_c)
    cos_head = jnp.concatenate([cos_r, cos_r, cos_c, cos_c], axis=1)
    sin_head = jnp.concatenate([-sin_r, sin_r, -sin_c, sin_c], axis=1)
    reps = LANES // HEAD_DIM
    cos, sin = jnp.tile(cos_head, (1, reps)), jnp.tile(sin_head, (1, reps))
    cos_meta = jnp.ones((META_ROWS, LANES), F32)
    sin_meta = jnp.zeros((META_ROWS, LANES), F32)
    return cos, sin, cos_meta, sin_meta


def kernel(x, meta_tokens, s5_lambda_re, s5_lambda_im, s5_log_dt, s5_b_re, s5_b_im, s5_c_re, s5_c_im, s5_d,
           s5_w_glu, s5_w_out, attn_w_qkv, attn_q_gain, attn_k_gain, attn_w_out, ffn_w_gate, ffn_w_up,
           ffn_w_down, ln_gain, ln_bias):
    hr = x.reshape(BATCH, N_CHUNKS, CHUNK, D_MODEL).transpose(2, 0, 1, 3).reshape(REAL_ROWS, D_MODEL)
    hm = jnp.broadcast_to(meta_tokens[None], (BATCH, N_META, D_MODEL)).reshape(META_ROWS, D_MODEL)
    cos, sin, cos_meta, sin_meta = _rope_tables()
    head_id = jnp.arange(LANES) // HEAD_DIM
    ones_blk = ((head_id[:, None] == head_id[None, :]).astype(F32) / HEAD_DIM).astype(BF16)

    for i in range(DEPTH):
        j = i // 2
        ln0 = [ln_gain[i, 0][None], ln_bias[i, 0][None]]
        ln1 = [ln_gain[i, 1][None], ln_bias[i, 1][None]]
        ffn_w = [ffn_w_gate[i].astype(BF16), ffn_w_up[i].astype(BF16), ffn_w_down[i].astype(BF16)]
        if i % 2 == 0:
            wb, wc, a, a16 = _s5_weights(s5_lambda_re[j], s5_lambda_im[j], s5_log_dt[j], s5_b_re[j], s5_b_im[j],
                                         s5_c_re[j], s5_c_im[j])
            (yf, yr), (ymf, ymr) = _s5_scan(hr, hm, wb, wc, a, a16)
            mix_w = [s5_d[j][None], s5_w_glu[j].astype(BF16), s5_w_out[j].astype(BF16)]
            hr, hm = _residual_layer_call(_s5_mix, [hr, yf, yr], [_flat_spec] * 3, [hm, ymf, ymr],
                                          mix_w, ln0, ffn_w, ln1, "s5_tail_ffn")
        else:
            wqkv = attn_w_qkv[j].astype(BF16)
            qg = jnp.tile(attn_q_gain[j], N_Q_HEADS)[None]
            kg = jnp.tile(attn_k_gain[j], N_KV_HEADS)[None]
            q, k, vt = _qkv(hr, wqkv, qg, kg, cos, sin, ones_blk, ROW_TILE, "qkv_real", True)
            qm, km, vm = _qkv(hm, wqkv, qg, kg, cos_meta, sin_meta, ones_blk, N_META, "qkv_meta", False)
            vtm = jnp.concatenate([vm.swapaxes(2, 3), jnp.ones((BATCH, N_KV_HEADS, V_ROWS - HEAD_DIM, N_META), BF16)],
                                  axis=2)
            o = _flash(q, k, vt, km, vtm, Q_TILE, "flash_real")
            om = _flash(qm, k, vt, km, vtm, N_META, "flash_meta").reshape(META_ROWS, D_MODEL)
            hr, hm = _residual_layer_call(_attn_mix, [hr, o], [_flat_spec, _batch_major_spec], [hm, om],
                                          [attn_w_out[j].astype(BF16)], ln0, ffn_w, ln1, "attn_tail_ffn")
    return hr.reshape(CHUNK, BATCH, N_CHUNKS, D_MODEL).transpose(1, 2, 0, 3).reshape(BATCH, SEQ, D_MODEL)
```

```python
import functools
import math

import jax
import jax.numpy as jnp
from jax import lax
from jax.experimental import pallas as pl
from jax.experimental.pallas import tpu as pltpu

D_MODEL = 1024
BATCH = 2
SEQ = 8192
DEPTH = 4
N_META = 16
GRID_W = 64
HEAD_DIM = 64
N_Q_HEADS = D_MODEL // HEAD_DIM
N_KV_HEADS = N_Q_HEADS // 4
Q_PER_KV = N_Q_HEADS // N_KV_HEADS
Q_WIDTH = N_Q_HEADS * HEAD_DIM
KV_WIDTH = N_KV_HEADS * HEAD_DIM
ROPE_THETA = 10000.0
ROPE_AXIS_DIM = HEAD_DIM // 2
QK_EPS = 1e-6
S5_GROUP_CH = 16
S5_GROUPS = D_MODEL // S5_GROUP_CH
S5_STATE = 64
D_FF = -(-8 * D_MODEL // (3 * 256)) * 256
LN_EPS = 1e-5
DEEPNORM_ALPHA = (2.0 * DEPTH) ** 0.25

LANES = 128
MXU_TILE = 256
CHUNK = 16
N_CHUNKS = SEQ // CHUNK
REAL_ROWS = BATCH * SEQ
CHUNK_ROWS = BATCH * N_CHUNKS
META_ROWS = BATCH * N_META
ROW_TILE = 512
SCAN_ROWS = 16
STATE_W = S5_GROUPS * S5_STATE
GROUPS_PER_TILE = MXU_TILE // S5_GROUP_CH
STATE_TILE = GROUPS_PER_TILE * S5_STATE
N_CH_TILES = D_MODEL // MXU_TILE
SCAN_LANES = 256
Q_TILE = 256
KEY_TILE = 1024
BF16_SUBLANES = 16
V_ROWS = HEAD_DIM + BF16_SUBLANES
LOG2_E = math.log2(math.e)
VMEM_LIMIT = 56 * 1024 * 1024

F32 = jnp.float32
BF16 = jnp.bfloat16


def _const_spec(a):
    return pl.BlockSpec(a.shape, lambda *_, nd=a.ndim: (0,) * nd, pipeline_mode=pl.Buffered(1))


def _params(n_axes):
    return pltpu.CompilerParams(dimension_semantics=("arbitrary",) * n_axes, vmem_limit_bytes=VMEM_LIMIT)


def _dot(a, b):
    return jnp.dot(a, b, preferred_element_type=F32)


def _layer_norm(z, gain, bias):
    mean = jnp.mean(z, axis=-1, keepdims=True)
    zc = z - mean
    var = jnp.mean(zc * zc, axis=-1, keepdims=True)
    return zc * lax.rsqrt(var + LN_EPS) * gain + bias


def _s5_mix(h, yf, yr, d_ref, wglu_ref, wout_ref):
    y = yf + yr + d_ref[...] * h
    g = 0.5 * y * (1.0 + lax.erf(y * math.sqrt(0.5)))
    zz = (g * jax.nn.sigmoid(_dot(g.astype(BF16), wglu_ref[...]))).astype(BF16)
    return _dot(zz, wout_ref[...])


def _attn_mix(h, o, wout_ref):
    return _dot(o, wout_ref[...])


def _residual_layer(mix, n_mix_consts, vals, const_refs):
    mix_refs = const_refs[:n_mix_consts]
    gain0, bias0, wg, wu, wd, gain1, bias1 = const_refs[n_mix_consts:]
    h = vals[0]
    h = _layer_norm(DEEPNORM_ALPHA * h + mix(*vals, *mix_refs), gain0[...], bias0[...])
    hb = h.astype(BF16)
    gate = _dot(hb, wg[...])
    up = _dot(hb, wu[...])
    act = (gate * jax.nn.sigmoid(gate) * up).astype(BF16)
    return _layer_norm(DEEPNORM_ALPHA * h + _dot(act, wd[...]), gain1[...], bias1[...])


def _residual_layer_body(*refs, mix, n_rows, n_mix_consts, n_tiles):
    real_refs, meta_refs = refs[:n_rows], refs[n_rows:2 * n_rows]
    const_refs = refs[2 * n_rows:-2]
    out_real, out_meta = refs[-2:]
    load = lambda r: r[0] if len(r.shape) == 3 else r[...]
    t = pl.program_id(0)

    @pl.when(t < n_tiles)
    def _():
        out_real[...] = _residual_layer(mix, n_mix_consts, [load(r) for r in real_refs], const_refs)

    @pl.when(t == n_tiles)
    def _():
        out_meta[...] = _residual_layer(mix, n_mix_consts, [load(r) for r in meta_refs], const_refs)


def _residual_layer_call(mix, real, real_specs, meta, mix_consts, ln0, ffn_consts, ln1, name):
    n_tiles = REAL_ROWS // ROW_TILE
    tile_of = lambda t: jnp.minimum(t, n_tiles - 1)
    consts = [*mix_consts, *ln0, *ffn_consts, *ln1]
    whole = lambda a: pl.BlockSpec(a.shape, lambda t, nd=a.ndim: (0,) * nd)
    body = functools.partial(_residual_layer_body, mix=mix, n_rows=len(real), n_mix_consts=len(mix_consts),
                             n_tiles=n_tiles)
    return pl.pallas_call(
        body,
        grid=(n_tiles + 1,),
        in_specs=[spec(tile_of) for spec in real_specs] + [whole(a) for a in meta] + [_const_spec(a) for a in consts],
        out_specs=[pl.BlockSpec((ROW_TILE, D_MODEL), lambda t: (tile_of(t), 0)),
                   pl.BlockSpec((META_ROWS, D_MODEL), lambda t: (0, 0))],
        out_shape=[jax.ShapeDtypeStruct((REAL_ROWS, D_MODEL), F32), jax.ShapeDtypeStruct((META_ROWS, D_MODEL), F32)],
        compiler_params=_params(1),
        name=name,
    )(*real, *meta, *consts)


def _flat_spec(tile_of):
    return pl.BlockSpec((ROW_TILE, D_MODEL), lambda t: (tile_of(t), 0))


def _batch_major_spec(tile_of):
    return pl.BlockSpec((1, ROW_TILE, D_MODEL), lambda t: (tile_of(t) % BATCH, tile_of(t) // BATCH, 0))


def _rms_rope(t, gain, cos, sin, ones_blk):
    lane = lax.broadcasted_iota(jnp.int32, (t.shape[0], LANES), 1)
    first_half = (lane % ROPE_AXIS_DIM) < (ROPE_AXIS_DIM // 2)
    cols = []
    for c in range(t.shape[1] // LANES):
        tc = t[:, c * LANES:(c + 1) * LANES]
        sq = tc * tc
        hi = sq.astype(BF16)
        lo = (sq - hi.astype(F32)).astype(BF16)
        ms = _dot(jnp.concatenate([hi, lo], axis=1), ones_blk)
        tn = tc * lax.rsqrt(ms + QK_EPS) * gain[:, c * LANES:(c + 1) * LANES]
        half = ROPE_AXIS_DIM // 2
        partner = jnp.where(first_half, pltpu.roll(tn, LANES - half, axis=1), pltpu.roll(tn, half, axis=1))
        cols.append(tn * cos + partner * sin)
    return jnp.concatenate(cols, axis=1)


def _qkv_body(h_ref, w_ref, qg_ref, kg_ref, cos_ref, sin_ref, ones_ref, q_ref, k_ref, v_ref, *, v_transposed):
    qkv = _dot(h_ref[...].astype(BF16), w_ref[...])
    cos, sin, ones_blk = cos_ref[...], sin_ref[...], ones_ref[...]
    q = _rms_rope(qkv[:, :Q_WIDTH], qg_ref[...], cos, sin, ones_blk)
    k = _rms_rope(qkv[:, Q_WIDTH:Q_WIDTH + KV_WIDTH], kg_ref[...], cos, sin, ones_blk)
    v = qkv[:, Q_WIDTH + KV_WIDTH:]
    q_ref[0] = (q * (HEAD_DIM ** -0.5 * LOG2_E)).astype(BF16)
    if v_transposed:
        vt = v.T
    for g in range(N_KV_HEADS):
        k_ref[0, g] = k[:, g * HEAD_DIM:(g + 1) * HEAD_DIM].astype(BF16)
        if v_transposed:
            v_ref[0, g, 0:HEAD_DIM, :] = vt[g * HEAD_DIM:(g + 1) * HEAD_DIM, :].astype(BF16)
            v_ref[0, g, HEAD_DIM:, :] = jnp.ones((V_ROWS - HEAD_DIM, vt.shape[1]), BF16)
        else:
            v_ref[0, g] = v[:, g * HEAD_DIM:(g + 1) * HEAD_DIM].astype(BF16)


def _qkv(h, w, qg, kg, cos, sin, ones_blk, tile, name, v_transposed):
    rows = h.shape[0]
    per_batch = rows // BATCH
    consts_a = [w, qg, kg]
    if v_transposed:
        v_spec = pl.BlockSpec((1, N_KV_HEADS, V_ROWS, tile), lambda t: (t % BATCH, 0, 0, t // BATCH))
        v_shape = jax.ShapeDtypeStruct((BATCH, N_KV_HEADS, V_ROWS, per_batch), BF16)
    else:
        v_spec = pl.BlockSpec((1, N_KV_HEADS, tile, HEAD_DIM), lambda t: (t % BATCH, 0, t // BATCH, 0))
        v_shape = jax.ShapeDtypeStruct((BATCH, N_KV_HEADS, per_batch, HEAD_DIM), BF16)
    return pl.pallas_call(
        functools.partial(_qkv_body, v_transposed=v_transposed),
        grid=(rows // tile,),
        in_specs=[pl.BlockSpec((tile, D_MODEL), lambda t: (t, 0))] + [_const_spec(a) for a in consts_a]
        + [pl.BlockSpec((tile, LANES), lambda t: (t, 0)), pl.BlockSpec((tile, LANES), lambda t: (t, 0)),
           _const_spec(ones_blk)],
        out_specs=[pl.BlockSpec((1, tile, Q_WIDTH), lambda t: (t % BATCH, t // BATCH, 0)),
                   pl.BlockSpec((1, N_KV_HEADS, tile, HEAD_DIM), lambda t: (t % BATCH, 0, t // BATCH, 0)),
                   v_spec],
        out_shape=[jax.ShapeDtypeStruct((BATCH, per_batch, Q_WIDTH), BF16),
                   jax.ShapeDtypeStruct((BATCH, N_KV_HEADS, per_batch, HEAD_DIM), BF16),
                   v_shape],
        compiler_params=_params(1),
        name=name,
    )(h, *consts_a, cos, sin, ones_blk)


def _flash_body(q_ref, qn_ref, k_ref, vt_ref, km_ref, vtm_ref, o_ref, qs_ref, qsn_ref, m_ref, acc_ref, sa_ref,
                sb_ref, *, tq, n_key_tiles):
    rows = Q_PER_KV * tq
    n_col_blocks = 2 if rows % (2 * LANES) == 0 else 1
    cw = rows // n_col_blocks

    def stack_heads(src_ref, dst_ref):
        qb = src_ref[0]
        for h in range(Q_PER_KV):
            dst_ref[h * tq:(h + 1) * tq, :] = qb[:, h * HEAD_DIM:(h + 1) * HEAD_DIM]

    def scores(k, c, stacked_ref=qs_ref):
        return lax.dot_general(k, stacked_ref[c * cw:(c + 1) * cw, :], (((1,), (1,)), ((), ())),
                               preferred_element_type=F32)

    @pl.when(pl.program_id(2) == 0)
    def _():
        stack_heads(q_ref, qs_ref)
        k_first = k_ref[0, 0, 0:KEY_TILE, :]
        for c in range(n_col_blocks):
            sa_ref[:, c * cw:(c + 1) * cw] = scores(k_first, c)

    @pl.when(pl.program_id(2) > 0)
    def _():
        qs_ref[...] = qsn_ref[...]

    stack_heads(qn_ref, qsn_ref)

    def absorb(st, vt, c):
        cols = slice(c * cw, (c + 1) * cw)
        m_prev = m_ref[:, cols]
        m_new = jnp.maximum(m_prev, jnp.max(st, axis=0, keepdims=True))
        alpha = jnp.exp2(m_prev - m_new)
        pt = jnp.exp2(st - m_new).astype(BF16)
        acc_ref[:, cols] = alpha * acc_ref[:, cols] + _dot(vt, pt)
        m_ref[:, cols] = m_new

    def key_span(j):
        start = j * KEY_TILE
        return pl.ds(start if isinstance(j, int) else pl.multiple_of(start, KEY_TILE), KEY_TILE)

    def k_tile(j):
        return k_ref[0, 0, key_span(j), :]

    def v_tile(j):
        return vt_ref[0, 0, :, key_span(j)]

    def step(j, cur_ref, next_ref, next_tile, stacked_ref=qs_ref):
        k_next = k_tile(next_tile)
        vt = v_tile(j)
        for c in range(n_col_blocks):
            cols = slice(c * cw, (c + 1) * cw)
            next_ref[:, cols] = scores(k_next, c, stacked_ref)
            absorb(cur_ref[:, cols], vt, c)

    stm = lax.dot_general(km_ref[0, 0], qs_ref[...], (((1,), (1,)), ((), ())), preferred_element_type=F32)
    m_meta = jnp.max(stm, axis=0, keepdims=True)
    m_ref[...] = m_meta
    acc_ref[...] = _dot(vtm_ref[0, 0], jnp.exp2(stm - m_meta).astype(BF16))

    def pair_step(jj, carry):
        step(2 * jj, sa_ref, sb_ref, 2 * jj + 1)
        step(2 * jj + 1, sb_ref, sa_ref, 2 * jj + 2)
        return carry

    lax.fori_loop(0, n_key_tiles // 2 - 1, pair_step, 0)
    step(n_key_tiles - 2, sa_ref, sb_ref, n_key_tiles - 1)
    step(n_key_tiles - 1, sb_ref, sa_ref, 0, qsn_ref)
    acc = acc_ref[...]
    out = (acc[:HEAD_DIM, :] / acc[HEAD_DIM:HEAD_DIM + 1, :]).T
    o_ref[0] = jnp.concatenate([out[h * tq:(h + 1) * tq, :] for h in range(Q_PER_KV)], axis=1).astype(BF16)


def _flash(q, k, vt, km, vtm, tq, name):
    lq = q.shape[1]
    rows = Q_PER_KV * tq
    n_q_tiles = lq // tq
    head = lambda b, g, t: (b, g, 0, 0)
    q_width = Q_PER_KV * HEAD_DIM
    return pl.pallas_call(
        functools.partial(_flash_body, tq=tq, n_key_tiles=SEQ // KEY_TILE),
        grid=(BATCH, N_KV_HEADS, n_q_tiles),
        in_specs=[pl.BlockSpec((1, tq, q_width), lambda b, g, t: (b, t, g)),
                  pl.BlockSpec((1, tq, q_width), lambda b, g, t: (b, jnp.minimum(t + 1, n_q_tiles - 1), g)),
                  pl.BlockSpec((1, 1, SEQ, HEAD_DIM), head), pl.BlockSpec((1, 1, V_ROWS, SEQ), head),
                  pl.BlockSpec((1, 1, N_META, HEAD_DIM), head), pl.BlockSpec((1, 1, V_ROWS, N_META), head)],
        out_specs=pl.BlockSpec((1, tq, q_width), lambda b, g, t: (b, t, g)),
        out_shape=jax.ShapeDtypeStruct((BATCH, lq, Q_WIDTH), BF16),
        scratch_shapes=[pltpu.VMEM((rows, HEAD_DIM), BF16), pltpu.VMEM((rows, HEAD_DIM), BF16),
                        pltpu.VMEM((1, rows), F32), pltpu.VMEM((V_ROWS, rows), F32),
                        pltpu.VMEM((KEY_TILE, rows), F32), pltpu.VMEM((KEY_TILE, rows), F32)],
        compiler_params=_params(3),
        name=name,
    )(q, q, k, vt, km, vtm)


def _complex_axpy(ar, ai, sr, si, br, bi):
    return ar * sr - ai * si + br, ar * si + ai * sr + bi


def _s5_sweep(bu_ref, a_ref, lc, sr, si, emit):
    ar = a_ref[0, :, lc:lc + SCAN_LANES]
    ai = a_ref[1, :, lc:lc + SCAN_LANES]
    for k in range(CHUNK):
        bur = bu_ref[0, k * SCAN_ROWS:(k + 1) * SCAN_ROWS, lc:lc + SCAN_LANES]
        bui = bu_ref[1, k * SCAN_ROWS:(k + 1) * SCAN_ROWS, lc:lc + SCAN_LANES]
        sr, si = _complex_axpy(ar, ai, sr, si, bur, bui)
        emit(k, sr, si)
    return sr, si


def _s5_dir_body(x_ref, init_ref, wb_ref, wc_ref, a_ref, a16_ref, y_ref, end_ref,
                 bu_ref, st_ref, loc_ref, sin_ref, carry_ref, *, reverse, chain_rows):
    b, j = pl.program_id(0), pl.program_id(1)
    order = range(CHUNK - 1, -1, -1) if reverse else range(CHUNK)
    if chain_rows:
        @pl.when(j == 0)
        def _():
            carry_ref[0:1, :] = init_ref[pl.ds(b, 1), :]

    xb = jnp.concatenate([x_ref[i] for i in order], axis=0).astype(BF16)
    for ct in range(N_CH_TILES):
        xct = xb[:, ct * MXU_TILE:(ct + 1) * MXU_TILE]
        yct = None
        for off in range(0, STATE_TILE, SCAN_LANES):
            lc = ct * STATE_TILE + off
            re = slice(lc, lc + SCAN_LANES)
            im = slice(STATE_W + lc, STATE_W + lc + SCAN_LANES)
            wcols = slice(off, off + SCAN_LANES)
            bu_ref[0, :, re] = _dot(xct, wb_ref[0, ct, :, wcols])
            bu_ref[1, :, re] = _dot(xct, wb_ref[1, ct, :, wcols])

            zero = jnp.zeros((SCAN_ROWS, SCAN_LANES), F32)
            lr, li = _s5_sweep(bu_ref, a_ref, lc, zero, zero, lambda k, sr, si: None)

            pr, pi = a16_ref[0, :, re], a16_ref[1, :, re]
            if chain_rows:
                loc_ref[:, re] = lr
                loc_ref[:, im] = li
                cr, ci = carry_ref[0:1, re], carry_ref[0:1, im]
                for r in order:
                    sin_ref[r:r + 1, re] = cr
                    sin_ref[r:r + 1, im] = ci
                    cr, ci = _complex_axpy(pr, pi, cr, ci, loc_ref[r:r + 1, re], loc_ref[r:r + 1, im])
                carry_ref[0:1, re] = cr
                carry_ref[0:1, im] = ci
                end_ref[0, :, re] = jnp.broadcast_to(cr, (end_ref.shape[1], SCAN_LANES))
                end_ref[0, :, im] = jnp.broadcast_to(ci, (end_ref.shape[1], SCAN_LANES))
                sr, si = sin_ref[:, re], sin_ref[:, im]
            else:
                sr, si = init_ref[:, re], init_ref[:, im]
                er, ei = _complex_axpy(pr, pi, sr, si, lr, li)
                end_ref[:, re] = er
                end_ref[:, im] = ei

            def emit(k, fr, fi, re=re):
                st_ref[0, k * SCAN_ROWS:(k + 1) * SCAN_ROWS, re] = fr.astype(BF16)
                st_ref[1, k * SCAN_ROWS:(k + 1) * SCAN_ROWS, re] = fi.astype(BF16)
            _s5_sweep(bu_ref, a_ref, lc, sr, si, emit)

            part = _dot(st_ref[0, :, re], wc_ref[0, ct, wcols, :]) + _dot(st_ref[1, :, re], wc_ref[1, ct, wcols, :])
            yct = part if yct is None else yct + part
        for k, i in enumerate(order):
            y_ref[i, :, ct * MXU_TILE:(ct + 1) * MXU_TILE] = yct[k * SCAN_ROWS:(k + 1) * SCAN_ROWS, :]


def _s5_dir(x3, init, wb, wc, a, a16, reverse, chain_rows, name):
    n_rows = x3.shape[1]
    if chain_rows:
        n_batch, n_blocks = BATCH, n_rows // (BATCH * SCAN_ROWS)
        end_shape, end_spec = (BATCH, 8, 2 * STATE_W), pl.BlockSpec((1, 8, 2 * STATE_W), lambda b, j: (b, 0, 0))
    else:
        n_batch, n_blocks = 1, n_rows // SCAN_ROWS
        end_shape, end_spec = init.shape, pl.BlockSpec(init.shape, lambda b, j: (0, 0))

    def row_block(b, j):
        return (0, b * n_blocks + (n_blocks - 1 - j if reverse else j), 0)

    consts = [init, wb, wc, a, a16]
    return pl.pallas_call(
        functools.partial(_s5_dir_body, reverse=reverse, chain_rows=chain_rows),
        grid=(n_batch, n_blocks),
        in_specs=[pl.BlockSpec((CHUNK, SCAN_ROWS, D_MODEL), row_block)] + [_const_spec(c) for c in consts],
        out_specs=[pl.BlockSpec((CHUNK, SCAN_ROWS, D_MODEL), row_block), end_spec],
        out_shape=[jax.ShapeDtypeStruct(x3.shape, F32), jax.ShapeDtypeStruct(end_shape, F32)],
        scratch_shapes=[pltpu.VMEM((2, CHUNK * SCAN_ROWS, STATE_W), F32),
                        pltpu.VMEM((2, CHUNK * SCAN_ROWS, STATE_W), BF16),
                        pltpu.VMEM((SCAN_ROWS, 2 * STATE_W), F32),
                        pltpu.VMEM((SCAN_ROWS, 2 * STATE_W), F32),
                        pltpu.VMEM((8, 2 * STATE_W), F32)],
        compiler_params=_params(2),
        name=name,
    )(x3, *consts)


def _s5_weights(lam_re, lam_im, log_dt, b_re, b_im, c_re, c_im):
    dt = jnp.exp(log_dt)[..., None]
    mag = jnp.exp(lam_re * dt)
    abr = mag * jnp.cos(lam_im * dt)
    abi = mag * jnp.sin(lam_im * dt)
    nr, ni = abr - 1.0, abi
    den = lam_re * lam_re + lam_im * lam_im
    cr = (nr * lam_re + ni * lam_im) / den
    ci = (ni * lam_re - nr * lam_im) / den
    bbr = cr[..., None] * b_re - ci[..., None] * b_im
    bbi = cr[..., None] * b_im + ci[..., None] * b_re
    ch_group = jnp.arange(MXU_TILE) // S5_GROUP_CH
    st_group = jnp.arange(STATE_TILE) // S5_STATE

    def blockdiag_in(m):
        m = m.reshape(2, N_CH_TILES, GROUPS_PER_TILE, S5_STATE, S5_GROUP_CH)
        row = m.transpose(0, 1, 4, 2, 3).reshape(2, N_CH_TILES, 1, S5_GROUP_CH, STATE_TILE)
        full = jnp.broadcast_to(row, (2, N_CH_TILES, GROUPS_PER_TILE, S5_GROUP_CH, STATE_TILE))
        full = full.reshape(2, N_CH_TILES, MXU_TILE, STATE_TILE)
        return jnp.where(ch_group[:, None] == st_group[None, :], full, 0.0)

    def blockdiag_out(m):
        m = m.reshape(2, N_CH_TILES, GROUPS_PER_TILE, S5_GROUP_CH, S5_STATE)
        col = m.transpose(0, 1, 2, 4, 3).reshape(2, N_CH_TILES, STATE_TILE, 1, S5_GROUP_CH)
        full = jnp.broadcast_to(col, (2, N_CH_TILES, STATE_TILE, GROUPS_PER_TILE, S5_GROUP_CH))
        full = full.reshape(2, N_CH_TILES, STATE_TILE, MXU_TILE)
        return jnp.where(st_group[:, None] == ch_group[None, :], full, 0.0)

    wb = jnp.stack([blockdiag_in(bbr), blockdiag_in(bbi)], axis=1).astype(BF16)
    wc = jnp.stack([blockdiag_out(c_re), blockdiag_out(-c_im)], axis=1).astype(BF16)
    a = jnp.stack([abr, abi], axis=1).reshape(2, 2, 1, STATE_W)
    pr, pi = abr, abi
    for _ in range(int(math.log2(CHUNK))):
        pr, pi = pr * pr - pi * pi, 2.0 * pr * pi
    a16 = jnp.stack([pr, pi], axis=1).reshape(2, 2, 1, STATE_W)
    return wb, wc, a, a16


def _s5_scan(hr, hm, wb, wc, a, a16):
    x3 = hr.reshape(CHUNK, CHUNK_ROWS, D_MODEL)
    xm3 = jnp.zeros((CHUNK, SCAN_ROWS, D_MODEL), F32).at[:, :BATCH].set(
        hm.reshape(BATCH, N_META, D_MODEL).transpose(1, 0, 2))
    zero_state = jnp.zeros((SCAN_ROWS, 2 * STATE_W), F32)
    fwd = lambda x, init, chain, name: _s5_dir(x, init, wb[0], wc[0], a[0], a16[0], False, chain, name)
    rev = lambda x, init, chain, name: _s5_dir(x, init, wb[1], wc[1], a[1], a16[1], True, chain, name)
    ymf3, meta_end = fwd(xm3, zero_state, False, "s5_fwd_meta")
    yf3, _ = fwd(x3, meta_end, True, "s5_fwd_real")
    yr3, real_end = rev(x3, zero_state, True, "s5_rev_real")
    ymr3, _ = rev(xm3, zero_state.at[:BATCH].set(real_end[:, 0]), False, "s5_rev_meta")
    flat_meta = lambda y3: y3[:, :BATCH].transpose(1, 0, 2).reshape(META_ROWS, D_MODEL)
    return ((yf3.reshape(REAL_ROWS, D_MODEL), yr3.reshape(REAL_ROWS, D_MODEL)),
            (flat_meta(ymf3), flat_meta(ymr3)))


def _rope_tables():
    slot = jnp.arange(CHUNK, dtype=jnp.int32)[:, None, None]
    chunk = jnp.arange(N_CHUNKS, dtype=jnp.int32)[None, None, :]
    tok = jnp.broadcast_to(chunk * CHUNK + slot, (CHUNK, BATCH, N_CHUNKS)).reshape(-1)
    row_id = (tok // GRID_W).astype(F32)
    col_id = (tok % GRID_W).astype(F32)
    inv_freq = ROPE_THETA ** (-jnp.arange(0, ROPE_AXIS_DIM, 2, dtype=F32) / ROPE_AXIS_DIM)
    ang_r = row_id[:, None] * inv_freq[None, :]
    ang_c = col_id[:, None] * inv_freq[None, :]
    cos_r, sin_r, cos_c, sin_c = jnp.cos(ang_r), jnp.sin(ang_r), jnp.cos(ang_c), jnp.sin(ang_c)
    cos_head = jnp.concatenate([cos_r, cos_r, cos_c, cos_c], axis=1)
    sin_head = jnp.concatenate([-sin_r, sin_r, -sin_c, sin_c], axis=1)
    reps = LANES // HEAD_DIM
    cos, sin = jnp.tile(cos_head, (1, reps)), jnp.tile(sin_head, (1, reps))
    cos_meta = jnp.ones((META_ROWS, LANES), F32)
    sin_meta = jnp.zeros((META_ROWS, LANES), F32)
    return cos, sin, cos_meta, sin_meta


def kernel(x, meta_tokens, s5_lambda_re, s5_lambda_im, s5_log_dt, s5_b_re, s5_b_im, s5_c_re, s5_c_im, s5_d,
           s5_w_glu, s5_w_out, attn_w_qkv, attn_q_gain, attn_k_gain, attn_w_out, ffn_w_gate, ffn_w_up,
           ffn_w_down, ln_gain, ln_bias):
    hr = x.reshape(BATCH, N_CHUNKS, CHUNK, D_MODEL).transpose(2, 0, 1, 3).reshape(REAL_ROWS, D_MODEL)
    hm = jnp.broadcast_to(meta_tokens[None], (BATCH, N_META, D_MODEL)).reshape(META_ROWS, D_MODEL)
    cos, sin, cos_meta, sin_meta = _rope_tables()
    head_id = jnp.arange(LANES) // HEAD_DIM
    same_head = ((head_id[:, None] == head_id[None, :]).astype(F32) / HEAD_DIM).astype(BF16)
    ones_blk = jnp.concatenate([same_head, same_head], axis=0)

    for i in range(DEPTH):
        j = i // 2
        ln0 = [ln_gain[i, 0][None], ln_bias[i, 0][None]]
        ln1 = [ln_gain[i, 1][None], ln_bias[i, 1][None]]
        ffn_w = [ffn_w_gate[i].astype(BF16), ffn_w_up[i].astype(BF16), ffn_w_down[i].astype(BF16)]
        if i % 2 == 0:
            wb, wc, a, a16 = _s5_weights(s5_lambda_re[j], s5_lambda_im[j], s5_log_dt[j], s5_b_re[j], s5_b_im[j],
                                         s5_c_re[j], s5_c_im[j])
            (yf, yr), (ymf, ymr) = _s5_scan(hr, hm, wb, wc, a, a16)
            mix_w = [s5_d[j][None], s5_w_glu[j].astype(BF16), s5_w_out[j].astype(BF16)]
            hr, hm = _residual_layer_call(_s5_mix, [hr, yf, yr], [_flat_spec] * 3, [hm, ymf, ymr],
                                          mix_w, ln0, ffn_w, ln1, "s5_tail_ffn")
        else:
            wqkv = attn_w_qkv[j].astype(BF16)
            qg = jnp.tile(attn_q_gain[j], N_Q_HEADS)[None]
            kg = jnp.tile(attn_k_gain[j], N_KV_HEADS)[None]
            q, k, vt = _qkv(hr, wqkv, qg, kg, cos, sin, ones_blk, ROW_TILE, "qkv_real", True)
            qm, km, vm = _qkv(hm, wqkv, qg, kg, cos_meta, sin_meta, ones_blk, N_META, "qkv_meta", False)
            vtm = jnp.concatenate([vm.swapaxes(2, 3), jnp.ones((BATCH, N_KV_HEADS, V_ROWS - HEAD_DIM, N_META), BF16)],
                                  axis=2)
            o = _flash(q, k, vt, km, vtm, Q_TILE, "flash_real")
            om = _flash(qm, k, vt, km, vtm, N_META, "flash_meta").reshape(META_ROWS, D_MODEL)
            hr, hm = _residual_layer_call(_attn_mix, [hr, o], [_flat_spec, _batch_major_spec], [hm, om],
                                          [attn_w_out[j].astype(BF16)], ln0, ffn_w, ln1, "attn_tail_ffn")
    return hr.reshape(CHUNK, BATCH, N_CHUNKS, D_MODEL).transpose(1, 2, 0, 3).reshape(BATCH, SEQ, D_MODEL)
```

```python
import functools
import math

import jax
import jax.numpy as jnp
from jax import lax
from jax.experimental import pallas as pl
from jax.experimental.pallas import tpu as pltpu

D_MODEL = 1024
BATCH = 2
SEQ = 8192
DEPTH = 4
N_META = 16
GRID_W = 64
HEAD_DIM = 64
N_Q_HEADS = D_MODEL // HEAD_DIM
N_KV_HEADS = N_Q_HEADS // 4
Q_PER_KV = N_Q_HEADS // N_KV_HEADS
Q_WIDTH = N_Q_HEADS * HEAD_DIM
KV_WIDTH = N_KV_HEADS * HEAD_DIM
ROPE_THETA = 10000.0
ROPE_AXIS_DIM = HEAD_DIM // 2
QK_EPS = 1e-6
S5_GROUP_CH = 16
S5_GROUPS = D_MODEL // S5_GROUP_CH
S5_STATE = 64
D_FF = -(-8 * D_MODEL // (3 * 256)) * 256
LN_EPS = 1e-5
DEEPNORM_ALPHA = (2.0 * DEPTH) ** 0.25

LANES = 128
MXU_TILE = 256
CHUNK = 16
N_CHUNKS = SEQ // CHUNK
REAL_ROWS = BATCH * SEQ
CHUNK_ROWS = BATCH * N_CHUNKS
META_ROWS = BATCH * N_META
ROW_TILE = 512
SCAN_ROWS = 16
STATE_W = S5_GROUPS * S5_STATE
GROUPS_PER_TILE = MXU_TILE // S5_GROUP_CH
STATE_TILE = GROUPS_PER_TILE * S5_STATE
N_CH_TILES = D_MODEL // MXU_TILE
SCAN_LANES = 256
Q_TILE = 512
KEY_TILE = 1024
V_ROWS = 2 * HEAD_DIM
LOG2_E = math.log2(math.e)
VMEM_LIMIT = 56 * 1024 * 1024

F32 = jnp.float32
BF16 = jnp.bfloat16


def _const_spec(a):
    return pl.BlockSpec(a.shape, lambda *_, nd=a.ndim: (0,) * nd, pipeline_mode=pl.Buffered(1))


def _params(n_axes):
    return pltpu.CompilerParams(dimension_semantics=("arbitrary",) * n_axes, vmem_limit_bytes=VMEM_LIMIT)


def _dot(a, b):
    return jnp.dot(a, b, preferred_element_type=F32)


def _layer_norm(z, gain, bias):
    mean = jnp.mean(z, axis=-1, keepdims=True)
    zc = z - mean
    var = jnp.mean(zc * zc, axis=-1, keepdims=True)
    return zc * lax.rsqrt(var + LN_EPS) * gain + bias


def _s5_mix(h, yf, yr, d_ref, wglu_ref, wout_ref):
    y = yf + yr + d_ref[...] * h
    g = 0.5 * y * (1.0 + lax.erf(y * math.sqrt(0.5)))
    zz = (g * jax.nn.sigmoid(_dot(g.astype(BF16), wglu_ref[...]))).astype(BF16)
    return _dot(zz, wout_ref[...])


def _attn_mix(h, o, wout_ref):
    return _dot(o, wout_ref[...])


def _residual_layer(mix, n_mix_consts, vals, const_refs):
    mix_refs = const_refs[:n_mix_consts]
    gain0, bias0, wg, wu, wd, gain1, bias1 = const_refs[n_mix_consts:]
    h = vals[0]
    h = _layer_norm(DEEPNORM_ALPHA * h + mix(*vals, *mix_refs), gain0[...], bias0[...])
    hb = h.astype(BF16)
    gate = _dot(hb, wg[...])
    up = _dot(hb, wu[...])
    act = (gate * jax.nn.sigmoid(gate) * up).astype(BF16)
    return _layer_norm(DEEPNORM_ALPHA * h + _dot(act, wd[...]), gain1[...], bias1[...])


def _residual_layer_body(*refs, mix, n_rows, n_mix_consts, n_tiles):
    real_refs, meta_refs = refs[:n_rows], refs[n_rows:2 * n_rows]
    const_refs = refs[2 * n_rows:-2]
    out_real, out_meta = refs[-2:]
    load = lambda r: r[0] if len(r.shape) == 3 else r[...]
    t = pl.program_id(0)

    @pl.when(t < n_tiles)
    def _():
        out_real[...] = _residual_layer(mix, n_mix_consts, [load(r) for r in real_refs], const_refs)

    @pl.when(t == n_tiles)
    def _():
        out_meta[...] = _residual_layer(mix, n_mix_consts, [load(r) for r in meta_refs], const_refs)


def _residual_layer_call(mix, real, real_specs, meta, mix_consts, ln0, ffn_consts, ln1, name):
    n_tiles = REAL_ROWS // ROW_TILE
    tile_of = lambda t: jnp.minimum(t, n_tiles - 1)
    consts = [*mix_consts, *ln0, *ffn_consts, *ln1]
    whole = lambda a: pl.BlockSpec(a.shape, lambda t, nd=a.ndim: (0,) * nd)
    body = functools.partial(_residual_layer_body, mix=mix, n_rows=len(real), n_mix_consts=len(mix_consts),
                             n_tiles=n_tiles)
    return pl.pallas_call(
        body,
        grid=(n_tiles + 1,),
        in_specs=[spec(tile_of) for spec in real_specs] + [whole(a) for a in meta] + [_const_spec(a) for a in consts],
        out_specs=[pl.BlockSpec((ROW_TILE, D_MODEL), lambda t: (tile_of(t), 0)),
                   pl.BlockSpec((META_ROWS, D_MODEL), lambda t: (0, 0))],
        out_shape=[jax.ShapeDtypeStruct((REAL_ROWS, D_MODEL), F32), jax.ShapeDtypeStruct((META_ROWS, D_MODEL), F32)],
        compiler_params=_params(1),
        name=name,
    )(*real, *meta, *consts)


def _flat_spec(tile_of):
    return pl.BlockSpec((ROW_TILE, D_MODEL), lambda t: (tile_of(t), 0))


def _batch_major_spec(tile_of):
    return pl.BlockSpec((1, ROW_TILE, D_MODEL), lambda t: (tile_of(t) % BATCH, tile_of(t) // BATCH, 0))


def _rms_rope(t, gain, cos, sin, ones_blk):
    lane = lax.broadcasted_iota(jnp.int32, (t.shape[0], LANES), 1)
    first_half = (lane % ROPE_AXIS_DIM) < (ROPE_AXIS_DIM // 2)
    cols = []
    for c in range(t.shape[1] // LANES):
        tc = t[:, c * LANES:(c + 1) * LANES]
        sq = tc * tc
        hi = sq.astype(BF16)
        lo = (sq - hi.astype(F32)).astype(BF16)
        ms = _dot(jnp.concatenate([hi, lo], axis=1), ones_blk)
        tn = tc * lax.rsqrt(ms + QK_EPS) * gain[:, c * LANES:(c + 1) * LANES]
        half = ROPE_AXIS_DIM // 2
        partner = jnp.where(first_half, pltpu.roll(tn, LANES - half, axis=1), pltpu.roll(tn, half, axis=1))
        cols.append(tn * cos + partner * sin)
    return jnp.concatenate(cols, axis=1)


def _qkv_body(h_ref, w_ref, qg_ref, kg_ref, cos_ref, sin_ref, ones_ref, q_ref, k_ref, v_ref, *, v_transposed):
    qkv = _dot(h_ref[...].astype(BF16), w_ref[...])
    cos, sin, ones_blk = cos_ref[...], sin_ref[...], ones_ref[...]
    q = _rms_rope(qkv[:, :Q_WIDTH], qg_ref[...], cos, sin, ones_blk)
    k = _rms_rope(qkv[:, Q_WIDTH:Q_WIDTH + KV_WIDTH], kg_ref[...], cos, sin, ones_blk)
    v = qkv[:, Q_WIDTH + KV_WIDTH:]
    q_ref[0] = (q * (HEAD_DIM ** -0.5 * LOG2_E)).astype(BF16)
    if v_transposed:
        vt = v.T
    for g in range(N_KV_HEADS):
        k_ref[0, g] = k[:, g * HEAD_DIM:(g + 1) * HEAD_DIM].astype(BF16)
        if v_transposed:
            v_ref[0, g, 0:HEAD_DIM, :] = vt[g * HEAD_DIM:(g + 1) * HEAD_DIM, :].astype(BF16)
            v_ref[0, g, HEAD_DIM:, :] = jnp.ones((V_ROWS - HEAD_DIM, vt.shape[1]), BF16)
        else:
            v_ref[0, g] = v[:, g * HEAD_DIM:(g + 1) * HEAD_DIM].astype(BF16)


def _qkv(h, w, qg, kg, cos, sin, ones_blk, tile, name, v_transposed):
    rows = h.shape[0]
    per_batch = rows // BATCH
    consts_a = [w, qg, kg]
    if v_transposed:
        v_spec = pl.BlockSpec((1, N_KV_HEADS, V_ROWS, tile), lambda t: (t % BATCH, 0, 0, t // BATCH))
        v_shape = jax.ShapeDtypeStruct((BATCH, N_KV_HEADS, V_ROWS, per_batch), BF16)
    else:
        v_spec = pl.BlockSpec((1, N_KV_HEADS, tile, HEAD_DIM), lambda t: (t % BATCH, 0, t // BATCH, 0))
        v_shape = jax.ShapeDtypeStruct((BATCH, N_KV_HEADS, per_batch, HEAD_DIM), BF16)
    return pl.pallas_call(
        functools.partial(_qkv_body, v_transposed=v_transposed),
        grid=(rows // tile,),
        in_specs=[pl.BlockSpec((tile, D_MODEL), lambda t: (t, 0))] + [_const_spec(a) for a in consts_a]
        + [pl.BlockSpec((tile, LANES), lambda t: (t, 0)), pl.BlockSpec((tile, LANES), lambda t: (t, 0)),
           _const_spec(ones_blk)],
        out_specs=[pl.BlockSpec((1, tile, Q_WIDTH), lambda t: (t % BATCH, t // BATCH, 0)),
                   pl.BlockSpec((1, N_KV_HEADS, tile, HEAD_DIM), lambda t: (t % BATCH, 0, t // BATCH, 0)),
                   v_spec],
        out_shape=[jax.ShapeDtypeStruct((BATCH, per_batch, Q_WIDTH), BF16),
                   jax.ShapeDtypeStruct((BATCH, N_KV_HEADS, per_batch, HEAD_DIM), BF16),
                   v_shape],
        compiler_params=_params(1),
        name=name,
    )(h, *consts_a, cos, sin, ones_blk)


def _flash_body(q_ref, qn_ref, k_ref, vt_ref, km_ref, vtm_ref, o_ref, qs_ref, qsn_ref, m_ref, acc_ref, sa_ref,
                sb_ref, *, tq, n_key_tiles):
    rows = Q_PER_KV * tq
    n_col_blocks = rows // MXU_TILE if rows % MXU_TILE == 0 else 1
    cw = rows // n_col_blocks

    def stack_heads(src_ref, dst_ref):
        qb = src_ref[0]
        for h in range(Q_PER_KV):
            dst_ref[h * tq:(h + 1) * tq, :] = qb[:, h * HEAD_DIM:(h + 1) * HEAD_DIM]

    def scores(k, c, stacked_ref=qs_ref):
        return lax.dot_general(k, stacked_ref[c * cw:(c + 1) * cw, :], (((1,), (1,)), ((), ())),
                               preferred_element_type=F32)

    @pl.when(pl.program_id(2) == 0)
    def _():
        stack_heads(q_ref, qs_ref)
        k_first = k_ref[0, 0, 0:KEY_TILE, :]
        for c in range(n_col_blocks):
            sa_ref[:, c * cw:(c + 1) * cw] = scores(k_first, c)

    @pl.when(pl.program_id(2) > 0)
    def _():
        qs_ref[...] = qsn_ref[...]

    stack_heads(qn_ref, qsn_ref)

    def absorb(st, vt, c):
        cols = slice(c * cw, (c + 1) * cw)
        m_prev = m_ref[:, cols]
        m_new = jnp.maximum(m_prev, jnp.max(st, axis=0, keepdims=True))
        alpha = jnp.exp2(m_prev - m_new)
        pt = jnp.exp2(st - m_new).astype(BF16)
        acc_ref[:, cols] = alpha * acc_ref[:, cols] + _dot(vt, pt)
        m_ref[:, cols] = m_new

    def key_span(j):
        start = j * KEY_TILE
        return pl.ds(start if isinstance(j, int) else pl.multiple_of(start, KEY_TILE), KEY_TILE)

    def k_tile(j):
        return k_ref[0, 0, key_span(j), :]

    def v_tile(j):
        return vt_ref[0, 0, :, key_span(j)]

    def step(j, cur_ref, next_ref, next_tile, stacked_ref=qs_ref):
        k_next = k_tile(next_tile)
        vt = v_tile(j)
        for c in range(n_col_blocks):
            cols = slice(c * cw, (c + 1) * cw)
            next_ref[:, cols] = scores(k_next, c, stacked_ref)
            absorb(cur_ref[:, cols], vt, c)

    stm = lax.dot_general(km_ref[0, 0], qs_ref[...], (((1,), (1,)), ((), ())), preferred_element_type=F32)
    m_meta = jnp.max(stm, axis=0, keepdims=True)
    m_ref[...] = m_meta
    acc_ref[...] = _dot(vtm_ref[0, 0], jnp.exp2(stm - m_meta).astype(BF16))

    def pair_step(jj, carry):
        step(2 * jj, sa_ref, sb_ref, 2 * jj + 1)
        step(2 * jj + 1, sb_ref, sa_ref, 2 * jj + 2)
        return carry

    lax.fori_loop(0, n_key_tiles // 2 - 1, pair_step, 0)
    step(n_key_tiles - 2, sa_ref, sb_ref, n_key_tiles - 1)
    step(n_key_tiles - 1, sb_ref, sa_ref, 0, qsn_ref)
    acc = acc_ref[...]
    out = (acc[:HEAD_DIM, :] / acc[HEAD_DIM:HEAD_DIM + 1, :]).T
    o_ref[0] = jnp.concatenate([out[h * tq:(h + 1) * tq, :] for h in range(Q_PER_KV)], axis=1).astype(BF16)


def _flash(q, k, vt, km, vtm, tq, name):
    lq = q.shape[1]
    rows = Q_PER_KV * tq
    n_q_tiles = lq // tq
    head = lambda b, g, t: (b, g, 0, 0)
    q_width = Q_PER_KV * HEAD_DIM
    return pl.pallas_call(
        functools.partial(_flash_body, tq=tq, n_key_tiles=SEQ // KEY_TILE),
        grid=(BATCH, N_KV_HEADS, n_q_tiles),
        in_specs=[pl.BlockSpec((1, tq, q_width), lambda b, g, t: (b, t, g)),
                  pl.BlockSpec((1, tq, q_width), lambda b, g, t: (b, jnp.minimum(t + 1, n_q_tiles - 1), g)),
                  pl.BlockSpec((1, 1, SEQ, HEAD_DIM), head), pl.BlockSpec((1, 1, V_ROWS, SEQ), head),
                  pl.BlockSpec((1, 1, N_META, HEAD_DIM), head), pl.BlockSpec((1, 1, V_ROWS, N_META), head)],
        out_specs=pl.BlockSpec((1, tq, q_width), lambda b, g, t: (b, t, g)),
        out_shape=jax.ShapeDtypeStruct((BATCH, lq, Q_WIDTH), BF16),
        scratch_shapes=[pltpu.VMEM((rows, HEAD_DIM), BF16), pltpu.VMEM((rows, HEAD_DIM), BF16),
                        pltpu.VMEM((1, rows), F32), pltpu.VMEM((V_ROWS, rows), F32),
                        pltpu.VMEM((KEY_TILE, rows), F32), pltpu.VMEM((KEY_TILE, rows), F32)],
        compiler_params=_params(3),
        name=name,
    )(q, q, k, vt, km, vtm)


def _complex_axpy(ar, ai, sr, si, br, bi):
    return ar * sr - ai * si + br, ar * si + ai * sr + bi


def _s5_sweep(bu_ref, a_ref, lc, sr, si, emit):
    ar = a_ref[0, :, lc:lc + SCAN_LANES]
    ai = a_ref[1, :, lc:lc + SCAN_LANES]
    for k in range(CHUNK):
        bur = bu_ref[0, k * SCAN_ROWS:(k + 1) * SCAN_ROWS, lc:lc + SCAN_LANES]
        bui = bu_ref[1, k * SCAN_ROWS:(k + 1) * SCAN_ROWS, lc:lc + SCAN_LANES]
        sr, si = _complex_axpy(ar, ai, sr, si, bur, bui)
        emit(k, sr, si)
    return sr, si


def _s5_dir_body(x_ref, init_ref, wb_ref, wc_ref, a_ref, a16_ref, y_ref, end_ref,
                 bu_ref, st_ref, loc_ref, sin_ref, carry_ref, *, reverse, chain_rows):
    b, j = pl.program_id(0), pl.program_id(1)
    order = range(CHUNK - 1, -1, -1) if reverse else range(CHUNK)
    if chain_rows:
        @pl.when(j == 0)
        def _():
            carry_ref[0:1, :] = init_ref[pl.ds(b, 1), :]

    xb = jnp.concatenate([x_ref[i] for i in order], axis=0).astype(BF16)
    for ct in range(N_CH_TILES):
        re = slice(ct * STATE_TILE, (ct + 1) * STATE_TILE)
        im = slice(STATE_W + ct * STATE_TILE, STATE_W + (ct + 1) * STATE_TILE)
        xct = xb[:, ct * MXU_TILE:(ct + 1) * MXU_TILE]
        bu_ref[0, :, re] = _dot(xct, wb_ref[0, ct])
        bu_ref[1, :, re] = _dot(xct, wb_ref[1, ct])

        for lc in range(re.start, re.stop, SCAN_LANES):
            zero = jnp.zeros((SCAN_ROWS, SCAN_LANES), F32)
            sr, si = _s5_sweep(bu_ref, a_ref, lc, zero, zero, lambda k, sr, si: None)
            loc_ref[:, lc:lc + SCAN_LANES] = sr
            loc_ref[:, STATE_W + lc:STATE_W + lc + SCAN_LANES] = si

        pr, pi = a16_ref[0, :, re], a16_ref[1, :, re]
        if chain_rows:
            cr, ci = carry_ref[0:1, re], carry_ref[0:1, im]
            for r in order:
                sin_ref[r:r + 1, re] = cr
                sin_ref[r:r + 1, im] = ci
                cr, ci = _complex_axpy(pr, pi, cr, ci, loc_ref[r:r + 1, re], loc_ref[r:r + 1, im])
            carry_ref[0:1, re] = cr
            carry_ref[0:1, im] = ci
            end_ref[0, :, re] = jnp.broadcast_to(cr, (end_ref.shape[1], STATE_TILE))
            end_ref[0, :, im] = jnp.broadcast_to(ci, (end_ref.shape[1], STATE_TILE))
        else:
            sr, si = init_ref[:, re], init_ref[:, im]
            sin_ref[:, re] = sr
            sin_ref[:, im] = si
            er, ei = _complex_axpy(pr, pi, sr, si, loc_ref[:, re], loc_ref[:, im])
            end_ref[:, re] = er
            end_ref[:, im] = ei

        for lc in range(re.start, re.stop, SCAN_LANES):
            def emit(k, sr, si, lc=lc):
                st_ref[0, k * SCAN_ROWS:(k + 1) * SCAN_ROWS, lc:lc + SCAN_LANES] = sr.astype(BF16)
                st_ref[1, k * SCAN_ROWS:(k + 1) * SCAN_ROWS, lc:lc + SCAN_LANES] = si.astype(BF16)
            _s5_sweep(bu_ref, a_ref, lc, sin_ref[:, lc:lc + SCAN_LANES],
                      sin_ref[:, STATE_W + lc:STATE_W + lc + SCAN_LANES], emit)

        yct = _dot(st_ref[0, :, re], wc_ref[0, ct]) + _dot(st_ref[1, :, re], wc_ref[1, ct])
        for k, i in enumerate(order):
            y_ref[i, :, ct * MXU_TILE:(ct + 1) * MXU_TILE] = yct[k * SCAN_ROWS:(k + 1) * SCAN_ROWS, :]


def _s5_dir(x3, init, wb, wc, a, a16, reverse, chain_rows, name):
    n_rows = x3.shape[1]
    if chain_rows:
        n_batch, n_blocks = BATCH, n_rows // (BATCH * SCAN_ROWS)
        end_shape, end_spec = (BATCH, 8, 2 * STATE_W), pl.BlockSpec((1, 8, 2 * STATE_W), lambda b, j: (b, 0, 0))
    else:
        n_batch, n_blocks = 1, n_rows // SCAN_ROWS
        end_shape, end_spec = init.shape, pl.BlockSpec(init.shape, lambda b, j: (0, 0))

    def row_block(b, j):
        return (0, b * n_blocks + (n_blocks - 1 - j if reverse else j), 0)

    consts = [init, wb, wc, a, a16]
    return pl.pallas_call(
        functools.partial(_s5_dir_body, reverse=reverse, chain_rows=chain_rows),
        grid=(n_batch, n_blocks),
        in_specs=[pl.BlockSpec((CHUNK, SCAN_ROWS, D_MODEL), row_block)] + [_const_spec(c) for c in consts],
        out_specs=[pl.BlockSpec((CHUNK, SCAN_ROWS, D_MODEL), row_block), end_spec],
        out_shape=[jax.ShapeDtypeStruct(x3.shape, F32), jax.ShapeDtypeStruct(end_shape, F32)],
        scratch_shapes=[pltpu.VMEM((2, CHUNK * SCAN_ROWS, STATE_W), F32),
                        pltpu.VMEM((2, CHUNK * SCAN_ROWS, STATE_W), BF16),
                        pltpu.VMEM((SCAN_ROWS, 2 * STATE_W), F32),
                        pltpu.VMEM((SCAN_ROWS, 2 * STATE_W), F32),
                        pltpu.VMEM((8, 2 * STATE_W), F32)],
        compiler_params=_params(2),
        name=name,
    )(x3, *consts)


def _s5_weights(lam_re, lam_im, log_dt, b_re, b_im, c_re, c_im):
    dt = jnp.exp(log_dt)[..., None]
    mag = jnp.exp(lam_re * dt)
    abr = mag * jnp.cos(lam_im * dt)
    abi = mag * jnp.sin(lam_im * dt)
    nr, ni = abr - 1.0, abi
    den = lam_re * lam_re + lam_im * lam_im
    cr = (nr * lam_re + ni * lam_im) / den
    ci = (ni * lam_re - nr * lam_im) / den
    bbr = cr[..., None] * b_re - ci[..., None] * b_im
    bbi = cr[..., None] * b_im + ci[..., None] * b_re
    ch_group = jnp.arange(MXU_TILE) // S5_GROUP_CH
    st_group = jnp.arange(STATE_TILE) // S5_STATE

    def blockdiag_in(m):
        m = m.reshape(2, N_CH_TILES, GROUPS_PER_TILE, S5_STATE, S5_GROUP_CH)
        row = m.transpose(0, 1, 4, 2, 3).reshape(2, N_CH_TILES, 1, S5_GROUP_CH, STATE_TILE)
        full = jnp.broadcast_to(row, (2, N_CH_TILES, GROUPS_PER_TILE, S5_GROUP_CH, STATE_TILE))
        full = full.reshape(2, N_CH_TILES, MXU_TILE, STATE_TILE)
        return jnp.where(ch_group[:, None] == st_group[None, :], full, 0.0)

    def blockdiag_out(m):
        m = m.reshape(2, N_CH_TILES, GROUPS_PER_TILE, S5_GROUP_CH, S5_STATE)
        col = m.transpose(0, 1, 2, 4, 3).reshape(2, N_CH_TILES, STATE_TILE, 1, S5_GROUP_CH)
        full = jnp.broadcast_to(col, (2, N_CH_TILES, STATE_TILE, GROUPS_PER_TILE, S5_GROUP_CH))
        full = full.reshape(2, N_CH_TILES, STATE_TILE, MXU_TILE)
        return jnp.where(st_group[:, None] == ch_group[None, :], full, 0.0)

    wb = jnp.stack([blockdiag_in(bbr), blockdiag_in(bbi)], axis=1).astype(BF16)
    wc = jnp.stack([blockdiag_out(c_re), blockdiag_out(-c_im)], axis=1).astype(BF16)
    a = jnp.stack([abr, abi], axis=1).reshape(2, 2, 1, STATE_W)
    pr, pi = abr, abi
    for _ in range(int(math.log2(CHUNK))):
        pr, pi = pr * pr - pi * pi, 2.0 * pr * pi
    a16 = jnp.stack([pr, pi], axis=1).reshape(2, 2, 1, STATE_W)
    return wb, wc, a, a16


def _s5_scan(hr, hm, wb, wc, a, a16):
    x3 = hr.reshape(CHUNK, CHUNK_ROWS, D_MODEL)
    xm3 = jnp.zeros((CHUNK, SCAN_ROWS, D_MODEL), F32).at[:, :BATCH].set(
        hm.reshape(BATCH, N_META, D_MODEL).transpose(1, 0, 2))
    zero_state = jnp.zeros((SCAN_ROWS, 2 * STATE_W), F32)
    fwd = lambda x, init, chain, name: _s5_dir(x, init, wb[0], wc[0], a[0], a16[0], False, chain, name)
    rev = lambda x, init, chain, name: _s5_dir(x, init, wb[1], wc[1], a[1], a16[1], True, chain, name)
    ymf3, meta_end = fwd(xm3, zero_state, False, "s5_fwd_meta")
    yf3, _ = fwd(x3, meta_end, True, "s5_fwd_real")
    yr3, real_end = rev(x3, zero_state, True, "s5_rev_real")
    ymr3, _ = rev(xm3, zero_state.at[:BATCH].set(real_end[:, 0]), False, "s5_rev_meta")
    flat_meta = lambda y3: y3[:, :BATCH].transpose(1, 0, 2).reshape(META_ROWS, D_MODEL)
    return ((yf3.reshape(REAL_ROWS, D_MODEL), yr3.reshape(REAL_ROWS, D_MODEL)),
            (flat_meta(ymf3), flat_meta(ymr3)))


def _rope_tables():
    slot = jnp.arange(CHUNK, dtype=jnp.int32)[:, None, None]
    chunk = jnp.arange(N_CHUNKS, dtype=jnp.int32)[None, None, :]
    tok = jnp.broadcast_to(chunk * CHUNK + slot, (CHUNK, BATCH, N_CHUNKS)).reshape(-1)
    row_id = (tok // GRID_W).astype(F32)
    col_id = (tok % GRID_W).astype(F32)
    inv_freq = ROPE_THETA ** (-jnp.arange(0, ROPE_AXIS_DIM, 2, dtype=F32) / ROPE_AXIS_DIM)
    ang_r = row_id[:, None] * inv_freq[None, :]
    ang_c = col_id[:, None] * inv_freq[None, :]
    cos_r, sin_r, cos_c, sin_c = jnp.cos(ang_r), jnp.sin(ang_r), jnp.cos(ang_c), jnp.sin(ang_c)
    cos_head = jnp.concatenate([cos_r, cos_r, cos_c, cos_c], axis=1)
    sin_head = jnp.concatenate([-sin_r, sin_r, -sin_c, sin_c], axis=1)
    reps = LANES // HEAD_DIM
    cos, sin = jnp.tile(cos_head, (1, reps)), jnp.tile(sin_head, (1, reps))
    cos_meta = jnp.ones((META_ROWS, LANES), F32)
    sin_meta = jnp.zeros((META_ROWS, LANES), F32)
    return cos, sin, cos_meta, sin_meta


def kernel(x, meta_tokens, s5_lambda_re, s5_lambda_im, s5_log_dt, s5_b_re, s5_b_im, s5_c_re, s5_c_im, s5_d,
           s5_w_glu, s5_w_out, attn_w_qkv, attn_q_gain, attn_k_gain, attn_w_out, ffn_w_gate, ffn_w_up,
           ffn_w_down, ln_gain, ln_bias):
    hr = x.reshape(BATCH, N_CHUNKS, CHUNK, D_MODEL).transpose(2, 0, 1, 3).reshape(REAL_ROWS, D_MODEL)
    hm = jnp.broadcast_to(meta_tokens[None], (BATCH, N_META, D_MODEL)).reshape(META_ROWS, D_MODEL)
    cos, sin, cos_meta, sin_meta = _rope_tables()
    head_id = jnp.arange(LANES) // HEAD_DIM
    same_head = ((head_id[:, None] == head_id[None, :]).astype(F32) / HEAD_DIM).astype(BF16)
    ones_blk = jnp.concatenate([same_head, same_head], axis=0)

    for i in range(DEPTH):
        j = i // 2
        ln0 = [ln_gain[i, 0][None], ln_bias[i, 0][None]]
        ln1 = [ln_gain[i, 1][None], ln_bias[i, 1][None]]
        ffn_w = [ffn_w_gate[i].astype(BF16), ffn_w_up[i].astype(BF16), ffn_w_down[i].astype(BF16)]
        if i % 2 == 0:
            wb, wc, a, a16 = _s5_weights(s5_lambda_re[j], s5_lambda_im[j], s5_log_dt[j], s5_b_re[j], s5_b_im[j],
                                         s5_c_re[j], s5_c_im[j])
            (yf, yr), (ymf, ymr) = _s5_scan(hr, hm, wb, wc, a, a16)
            mix_w = [s5_d[j][None], s5_w_glu[j].astype(BF16), s5_w_out[j].astype(BF16)]
            hr, hm = _residual_layer_call(_s5_mix, [hr, yf, yr], [_flat_spec] * 3, [hm, ymf, ymr],
                                          mix_w, ln0, ffn_w, ln1, "s5_tail_ffn")
        else:
            wqkv = attn_w_qkv[j].astype(BF16)
            qg = jnp.tile(attn_q_gain[j], N_Q_HEADS)[None]
            kg = jnp.tile(attn_k_gain[j], N_KV_HEADS)[None]
            q, k, vt = _qkv(hr, wqkv, qg, kg, cos, sin, ones_blk, ROW_TILE, "qkv_real", True)
            qm, km, vm = _qkv(hm, wqkv, qg, kg, cos_meta, sin_meta, ones_blk, N_META, "qkv_meta", False)
            vtm = jnp.concatenate([vm.swapaxes(2, 3), jnp.ones((BATCH, N_KV_HEADS, V_ROWS - HEAD_DIM, N_META), BF16)],
                                  axis=2)
            o = _flash(q, k, vt, km, vtm, Q_TILE, "flash_real")
            om = _flash(qm, k, vt, km, vtm, N_META, "flash_meta").reshape(META_ROWS, D_MODEL)
            hr, hm = _residual_layer_call(_attn_mix, [hr, o], [_flat_spec, _batch_major_spec], [hm, om],
                                          [attn_w_out[j].astype(BF16)], ln0, ffn_w, ln1, "attn_tail_ffn")
    return hr.reshape(CHUNK, BATCH, N_CHUNKS, D_MODEL).transpose(1, 2, 0, 3).reshape(BATCH, SEQ, D_MODEL)
```

```python
import functools
import math

import jax
import jax.numpy as jnp
from jax import lax
from jax.experimental import pallas as pl
from jax.experimental.pallas import tpu as pltpu

D_MODEL = 1024
BATCH = 2
SEQ = 8192
DEPTH = 4
N_META = 16
GRID_W = 64
HEAD_DIM = 64
N_Q_HEADS = D_MODEL // HEAD_DIM
N_KV_HEADS = N_Q_HEADS // 4
Q_PER_KV = N_Q_HEADS // N_KV_HEADS
Q_WIDTH = N_Q_HEADS * HEAD_DIM
KV_WIDTH = N_KV_HEADS * HEAD_DIM
ROPE_THETA = 10000.0
ROPE_AXIS_DIM = HEAD_DIM // 2
QK_EPS = 1e-6
S5_GROUP_CH = 16
S5_GROUPS = D_MODEL // S5_GROUP_CH
S5_STATE = 64
D_FF = -(-8 * D_MODEL // (3 * 256)) * 256
LN_EPS = 1e-5
DEEPNORM_ALPHA = (2.0 * DEPTH) ** 0.25

LANES = 128
MXU_TILE = 256
CHUNK = 16
N_CHUNKS = SEQ // CHUNK
REAL_ROWS = BATCH * SEQ
CHUNK_ROWS = BATCH * N_CHUNKS
META_ROWS = BATCH * N_META
ROW_TILE = 512
SCAN_ROWS = 16
STATE_W = S5_GROUPS * S5_STATE
GROUPS_PER_TILE = MXU_TILE // S5_GROUP_CH
STATE_TILE = GROUPS_PER_TILE * S5_STATE
N_CH_TILES = D_MODEL // MXU_TILE
SCAN_LANES = 256
Q_TILE = 1024
KEY_TILE = 512
V_ROWS = 2 * HEAD_DIM
LOG2_E = math.log2(math.e)
VMEM_LIMIT = 56 * 1024 * 1024

F32 = jnp.float32
BF16 = jnp.bfloat16


def _const_spec(a):
    return pl.BlockSpec(a.shape, lambda *_, nd=a.ndim: (0,) * nd, pipeline_mode=pl.Buffered(1))


def _params(n_axes):
    return pltpu.CompilerParams(dimension_semantics=("arbitrary",) * n_axes, vmem_limit_bytes=VMEM_LIMIT)


def _dot(a, b):
    return jnp.dot(a, b, preferred_element_type=F32)


def _layer_norm(z, gain, bias):
    mean = jnp.mean(z, axis=-1, keepdims=True)
    zc = z - mean
    var = jnp.mean(zc * zc, axis=-1, keepdims=True)
    return zc * lax.rsqrt(var + LN_EPS) * gain + bias


def _s5_mix(h, yf, yr, d_ref, wglu_ref, wout_ref):
    y = yf + yr + d_ref[...] * h
    g = 0.5 * y * (1.0 + lax.erf(y * math.sqrt(0.5)))
    zz = (g * jax.nn.sigmoid(_dot(g.astype(BF16), wglu_ref[...]))).astype(BF16)
    return _dot(zz, wout_ref[...])


def _attn_mix(h, o, wout_ref):
    return _dot(o, wout_ref[...])


def _residual_layer(mix, n_mix_consts, vals, const_refs):
    mix_refs = const_refs[:n_mix_consts]
    gain0, bias0, wg, wu, wd, gain1, bias1 = const_refs[n_mix_consts:]
    h = vals[0]
    h = _layer_norm(DEEPNORM_ALPHA * h + mix(*vals, *mix_refs), gain0[...], bias0[...])
    hb = h.astype(BF16)
    gate = _dot(hb, wg[...])
    up = _dot(hb, wu[...])
    act = (gate * jax.nn.sigmoid(gate) * up).astype(BF16)
    return _layer_norm(DEEPNORM_ALPHA * h + _dot(act, wd[...]), gain1[...], bias1[...])


def _residual_layer_body(*refs, mix, n_rows, n_mix_consts, n_tiles):
    real_refs, meta_refs = refs[:n_rows], refs[n_rows:2 * n_rows]
    const_refs = refs[2 * n_rows:-2]
    out_real, out_meta = refs[-2:]
    load = lambda r: r[0] if len(r.shape) == 3 else r[...]
    t = pl.program_id(0)

    @pl.when(t < n_tiles)
    def _():
        out_real[...] = _residual_layer(mix, n_mix_consts, [load(r) for r in real_refs], const_refs)

    @pl.when(t == n_tiles)
    def _():
        out_meta[...] = _residual_layer(mix, n_mix_consts, [load(r) for r in meta_refs], const_refs)


def _residual_layer_call(mix, real, real_specs, meta, mix_consts, ln0, ffn_consts, ln1, name):
    n_tiles = REAL_ROWS // ROW_TILE
    tile_of = lambda t: jnp.minimum(t, n_tiles - 1)
    consts = [*mix_consts, *ln0, *ffn_consts, *ln1]
    whole = lambda a: pl.BlockSpec(a.shape, lambda t, nd=a.ndim: (0,) * nd)
    body = functools.partial(_residual_layer_body, mix=mix, n_rows=len(real), n_mix_consts=len(mix_consts),
                             n_tiles=n_tiles)
    return pl.pallas_call(
        body,
        grid=(n_tiles + 1,),
        in_specs=[spec(tile_of) for spec in real_specs] + [whole(a) for a in meta] + [_const_spec(a) for a in consts],
        out_specs=[pl.BlockSpec((ROW_TILE, D_MODEL), lambda t: (tile_of(t), 0)),
                   pl.BlockSpec((META_ROWS, D_MODEL), lambda t: (0, 0))],
        out_shape=[jax.ShapeDtypeStruct((REAL_ROWS, D_MODEL), F32), jax.ShapeDtypeStruct((META_ROWS, D_MODEL), F32)],
        compiler_params=_params(1),
        name=name,
    )(*real, *meta, *consts)


def _flat_spec(tile_of):
    return pl.BlockSpec((ROW_TILE, D_MODEL), lambda t: (tile_of(t), 0))


def _batch_major_spec(tile_of):
    return pl.BlockSpec((1, ROW_TILE, D_MODEL), lambda t: (tile_of(t) % BATCH, tile_of(t) // BATCH, 0))


def _rms_rope(t, gain, cos, sin, ones_blk):
    lane = lax.broadcasted_iota(jnp.int32, (t.shape[0], LANES), 1)
    first_half = (lane % ROPE_AXIS_DIM) < (ROPE_AXIS_DIM // 2)
    cols = []
    for c in range(t.shape[1] // LANES):
        tc = t[:, c * LANES:(c + 1) * LANES]
        sq = tc * tc
        hi = sq.astype(BF16)
        lo = (sq - hi.astype(F32)).astype(BF16)
        ms = _dot(jnp.concatenate([hi, lo], axis=1), ones_blk)
        tn = tc * lax.rsqrt(ms + QK_EPS) * gain[:, c * LANES:(c + 1) * LANES]
        half = ROPE_AXIS_DIM // 2
        partner = jnp.where(first_half, pltpu.roll(tn, LANES - half, axis=1), pltpu.roll(tn, half, axis=1))
        cols.append(tn * cos + partner * sin)
    return jnp.concatenate(cols, axis=1)


def _qkv_body(h_ref, w_ref, qg_ref, kg_ref, cos_ref, sin_ref, ones_ref, q_ref, k_ref, v_ref, *, v_transposed):
    qkv = _dot(h_ref[...].astype(BF16), w_ref[...])
    cos, sin, ones_blk = cos_ref[...], sin_ref[...], ones_ref[...]
    q = _rms_rope(qkv[:, :Q_WIDTH], qg_ref[...], cos, sin, ones_blk)
    k = _rms_rope(qkv[:, Q_WIDTH:Q_WIDTH + KV_WIDTH], kg_ref[...], cos, sin, ones_blk)
    v = qkv[:, Q_WIDTH + KV_WIDTH:]
    q_ref[0] = (q * (HEAD_DIM ** -0.5 * LOG2_E)).astype(BF16)
    if v_transposed:
        vt = v.T
    for g in range(N_KV_HEADS):
        k_ref[0, g] = k[:, g * HEAD_DIM:(g + 1) * HEAD_DIM].astype(BF16)
        if v_transposed:
            v_ref[0, g, 0:HEAD_DIM, :] = vt[g * HEAD_DIM:(g + 1) * HEAD_DIM, :].astype(BF16)
            v_ref[0, g, HEAD_DIM:, :] = jnp.ones((V_ROWS - HEAD_DIM, vt.shape[1]), BF16)
        else:
            v_ref[0, g] = v[:, g * HEAD_DIM:(g + 1) * HEAD_DIM].astype(BF16)


def _qkv(h, w, qg, kg, cos, sin, ones_blk, tile, name, v_transposed):
    rows = h.shape[0]
    per_batch = rows // BATCH
    consts_a = [w, qg, kg]
    if v_transposed:
        v_spec = pl.BlockSpec((1, N_KV_HEADS, V_ROWS, tile), lambda t: (t % BATCH, 0, 0, t // BATCH))
        v_shape = jax.ShapeDtypeStruct((BATCH, N_KV_HEADS, V_ROWS, per_batch), BF16)
    else:
        v_spec = pl.BlockSpec((1, N_KV_HEADS, tile, HEAD_DIM), lambda t: (t % BATCH, 0, t // BATCH, 0))
        v_shape = jax.ShapeDtypeStruct((BATCH, N_KV_HEADS, per_batch, HEAD_DIM), BF16)
    return pl.pallas_call(
        functools.partial(_qkv_body, v_transposed=v_transposed),
        grid=(rows // tile,),
        in_specs=[pl.BlockSpec((tile, D_MODEL), lambda t: (t, 0))] + [_const_spec(a) for a in consts_a]
        + [pl.BlockSpec((tile, LANES), lambda t: (t, 0)), pl.BlockSpec((tile, LANES), lambda t: (t, 0)),
           _const_spec(ones_blk)],
        out_specs=[pl.BlockSpec((1, tile, Q_WIDTH), lambda t: (t % BATCH, t // BATCH, 0)),
                   pl.BlockSpec((1, N_KV_HEADS, tile, HEAD_DIM), lambda t: (t % BATCH, 0, t // BATCH, 0)),
                   v_spec],
        out_shape=[jax.ShapeDtypeStruct((BATCH, per_batch, Q_WIDTH), BF16),
                   jax.ShapeDtypeStruct((BATCH, N_KV_HEADS, per_batch, HEAD_DIM), BF16),
                   v_shape],
        compiler_params=_params(1),
        name=name,
    )(h, *consts_a, cos, sin, ones_blk)


def _flash_body(q_ref, qn_ref, k_ref, vt_ref, km_ref, vtm_ref, o_ref, qs_ref, qsn_ref, m_ref, acc_ref, sa_ref,
                sb_ref, *, tq, n_key_tiles):
    rows = Q_PER_KV * tq
    n_col_blocks = rows // MXU_TILE if rows % MXU_TILE == 0 else 1
    cw = rows // n_col_blocks

    def stack_heads(src_ref, dst_ref):
        qb = src_ref[0]
        for h in range(Q_PER_KV):
            dst_ref[h * tq:(h + 1) * tq, :] = qb[:, h * HEAD_DIM:(h + 1) * HEAD_DIM]

    def scores(k, c, stacked_ref=qs_ref):
        return lax.dot_general(k, stacked_ref[c * cw:(c + 1) * cw, :], (((1,), (1,)), ((), ())),
                               preferred_element_type=F32)

    @pl.when(pl.program_id(2) == 0)
    def _():
        stack_heads(q_ref, qs_ref)
        k_first = k_ref[0, 0, 0:KEY_TILE, :]
        for c in range(n_col_blocks):
            sa_ref[:, c * cw:(c + 1) * cw] = scores(k_first, c)

    @pl.when(pl.program_id(2) > 0)
    def _():
        qs_ref[...] = qsn_ref[...]

    stack_heads(qn_ref, qsn_ref)

    def absorb(st, vt, c):
        cols = slice(c * cw, (c + 1) * cw)
        m_prev = m_ref[:, cols]
        m_new = jnp.maximum(m_prev, jnp.max(st, axis=0, keepdims=True))
        alpha = jnp.exp2(m_prev - m_new)
        pt = jnp.exp2(st - m_new).astype(BF16)
        acc_ref[:, cols] = alpha * acc_ref[:, cols] + _dot(vt, pt)
        m_ref[:, cols] = m_new

    def key_span(j):
        start = j * KEY_TILE
        return pl.ds(start if isinstance(j, int) else pl.multiple_of(start, KEY_TILE), KEY_TILE)

    def k_tile(j):
        return k_ref[0, 0, key_span(j), :]

    def v_tile(j):
        return vt_ref[0, 0, :, key_span(j)]

    def step(j, cur_ref, next_ref, next_tile, stacked_ref=qs_ref):
        k_next = k_tile(next_tile)
        vt = v_tile(j)
        for c in range(n_col_blocks):
            cols = slice(c * cw, (c + 1) * cw)
            next_ref[:, cols] = scores(k_next, c, stacked_ref)
            absorb(cur_ref[:, cols], vt, c)

    stm = lax.dot_general(km_ref[0, 0], qs_ref[...], (((1,), (1,)), ((), ())), preferred_element_type=F32)
    m_meta = jnp.max(stm, axis=0, keepdims=True)
    m_ref[...] = m_meta
    acc_ref[...] = _dot(vtm_ref[0, 0], jnp.exp2(stm - m_meta).astype(BF16))

    def pair_step(jj, carry):
        step(2 * jj, sa_ref, sb_ref, 2 * jj + 1)
        step(2 * jj + 1, sb_ref, sa_ref, 2 * jj + 2)
        return carry

    lax.fori_loop(0, n_key_tiles // 2 - 1, pair_step, 0)
    step(n_key_tiles - 2, sa_ref, sb_ref, n_key_tiles - 1)
    step(n_key_tiles - 1, sb_ref, sa_ref, 0, qsn_ref)
    acc = acc_ref[...]
    out = (acc[:HEAD_DIM, :] / acc[HEAD_DIM:HEAD_DIM + 1, :]).T
    o_ref[0] = jnp.concatenate([out[h * tq:(h + 1) * tq, :] for h in range(Q_PER_KV)], axis=1).astype(BF16)


def _flash(q, k, vt, km, vtm, tq, name):
    lq = q.shape[1]
    rows = Q_PER_KV * tq
    n_q_tiles = lq // tq
    head = lambda b, g, t: (b, g, 0, 0)
    q_width = Q_PER_KV * HEAD_DIM
    return pl.pallas_call(
        functools.partial(_flash_body, tq=tq, n_key_tiles=SEQ // KEY_TILE),
        grid=(BATCH, N_KV_HEADS, n_q_tiles),
        in_specs=[pl.BlockSpec((1, tq, q_width), lambda b, g, t: (b, t, g)),
                  pl.BlockSpec((1, tq, q_width), lambda b, g, t: (b, jnp.minimum(t + 1, n_q_tiles - 1), g)),
                  pl.BlockSpec((1, 1, SEQ, HEAD_DIM), head), pl.BlockSpec((1, 1, V_ROWS, SEQ), head),
                  pl.BlockSpec((1, 1, N_META, HEAD_DIM), head), pl.BlockSpec((1, 1, V_ROWS, N_META), head)],
        out_specs=pl.BlockSpec((1, tq, q_width), lambda b, g, t: (b, t, g)),
        out_shape=jax.ShapeDtypeStruct((BATCH, lq, Q_WIDTH), BF16),
        scratch_shapes=[pltpu.VMEM((rows, HEAD_DIM), BF16), pltpu.VMEM((rows, HEAD_DIM), BF16),
                        pltpu.VMEM((1, rows), F32), pltpu.VMEM((V_ROWS, rows), F32),
                        pltpu.VMEM((KEY_TILE, rows), F32), pltpu.VMEM((KEY_TILE, rows), F32)],
        compiler_params=_params(3),
        name=name,
    )(q, q, k, vt, km, vtm)


def _complex_axpy(ar, ai, sr, si, br, bi):
    return ar * sr - ai * si + br, ar * si + ai * sr + bi


def _s5_sweep(bu_ref, a_ref, lc, sr, si, emit):
    ar = a_ref[0, :, lc:lc + SCAN_LANES]
    ai = a_ref[1, :, lc:lc + SCAN_LANES]
    for k in range(CHUNK):
        bur = bu_ref[0, k * SCAN_ROWS:(k + 1) * SCAN_ROWS, lc:lc + SCAN_LANES]
        bui = bu_ref[1, k * SCAN_ROWS:(k + 1) * SCAN_ROWS, lc:lc + SCAN_LANES]
        sr, si = _complex_axpy(ar, ai, sr, si, bur, bui)
        emit(k, sr, si)
    return sr, si


def _s5_dir_body(x_ref, init_ref, wb_ref, wc_ref, a_ref, a16_ref, y_ref, end_ref,
                 bu_ref, st_ref, loc_ref, sin_ref, carry_ref, *, reverse, chain_rows):
    b, j = pl.program_id(0), pl.program_id(1)
    order = range(CHUNK - 1, -1, -1) if reverse else range(CHUNK)
    if chain_rows:
        @pl.when(j == 0)
        def _():
            carry_ref[0:1, :] = init_ref[pl.ds(b, 1), :]

    xb = jnp.concatenate([x_ref[i] for i in order], axis=0).astype(BF16)
    for ct in range(N_CH_TILES):
        re = slice(ct * STATE_TILE, (ct + 1) * STATE_TILE)
        im = slice(STATE_W + ct * STATE_TILE, STATE_W + (ct + 1) * STATE_TILE)
        xct = xb[:, ct * MXU_TILE:(ct + 1) * MXU_TILE]
        bu_ref[0, :, re] = _dot(xct, wb_ref[0, ct])
        bu_ref[1, :, re] = _dot(xct, wb_ref[1, ct])

        for lc in range(re.start, re.stop, SCAN_LANES):
            zero = jnp.zeros((SCAN_ROWS, SCAN_LANES), F32)
            sr, si = _s5_sweep(bu_ref, a_ref, lc, zero, zero, lambda k, sr, si: None)
            loc_ref[:, lc:lc + SCAN_LANES] = sr
            loc_ref[:, STATE_W + lc:STATE_W + lc + SCAN_LANES] = si

        pr, pi = a16_ref[0, :, re], a16_ref[1, :, re]
        if chain_rows:
            cr, ci = carry_ref[0:1, re], carry_ref[0:1, im]
            for r in order:
                sin_ref[r:r + 1, re] = cr
                sin_ref[r:r + 1, im] = ci
                cr, ci = _complex_axpy(pr, pi, cr, ci, loc_ref[r:r + 1, re], loc_ref[r:r + 1, im])
            carry_ref[0:1, re] = cr
            carry_ref[0:1, im] = ci
            end_ref[0, :, re] = jnp.broadcast_to(cr, (end_ref.shape[1], STATE_TILE))
            end_ref[0, :, im] = jnp.broadcast_to(ci, (end_ref.shape[1], STATE_TILE))
        else:
            sr, si = init_ref[:, re], init_ref[:, im]
            sin_ref[:, re] = sr
            sin_ref[:, im] = si
            er, ei = _complex_axpy(pr, pi, sr, si, loc_ref[:, re], loc_ref[:, im])
            end_ref[:, re] = er
            end_ref[:, im] = ei

        for lc in range(re.start, re.stop, SCAN_LANES):
            def emit(k, sr, si, lc=lc):
                st_ref[0, k * SCAN_ROWS:(k + 1) * SCAN_ROWS, lc:lc + SCAN_LANES] = sr.astype(BF16)
                st_ref[1, k * SCAN_ROWS:(k + 1) * SCAN_ROWS, lc:lc + SCAN_LANES] = si.astype(BF16)
            _s5_sweep(bu_ref, a_ref, lc, sin_ref[:, lc:lc + SCAN_LANES],
                      sin_ref[:, STATE_W + lc:STATE_W + lc + SCAN_LANES], emit)

        yct = _dot(st_ref[0, :, re], wc_ref[0, ct]) + _dot(st_ref[1, :, re], wc_ref[1, ct])
        for k, i in enumerate(order):
            y_ref[i, :, ct * MXU_TILE:(ct + 1) * MXU_TILE] = yct[k * SCAN_ROWS:(k + 1) * SCAN_ROWS, :]


def _s5_dir(x3, init, wb, wc, a, a16, reverse, chain_rows, name):
    n_rows = x3.shape[1]
    if chain_rows:
        n_batch, n_blocks = BATCH, n_rows // (BATCH * SCAN_ROWS)
        end_shape, end_spec = (BATCH, 8, 2 * STATE_W), pl.BlockSpec((1, 8, 2 * STATE_W), lambda b, j: (b, 0, 0))
    else:
        n_batch, n_blocks = 1, n_rows // SCAN_ROWS
        end_shape, end_spec = init.shape, pl.BlockSpec(init.shape, lambda b, j: (0, 0))

    def row_block(b, j):
        return (0, b * n_blocks + (n_blocks - 1 - j if reverse else j), 0)

    consts = [init, wb, wc, a, a16]
    return pl.pallas_call(
        functools.partial(_s5_dir_body, reverse=reverse, chain_rows=chain_rows),
        grid=(n_batch, n_blocks),
        in_specs=[pl.BlockSpec((CHUNK, SCAN_ROWS, D_MODEL), row_block)] + [_const_spec(c) for c in consts],
        out_specs=[pl.BlockSpec((CHUNK, SCAN_ROWS, D_MODEL), row_block), end_spec],
        out_shape=[jax.ShapeDtypeStruct(x3.shape, F32), jax.ShapeDtypeStruct(end_shape, F32)],
        scratch_shapes=[pltpu.VMEM((2, CHUNK * SCAN_ROWS, STATE_W), F32),
                        pltpu.VMEM((2, CHUNK * SCAN_ROWS, STATE_W), BF16),
                        pltpu.VMEM((SCAN_ROWS, 2 * STATE_W), F32),
                        pltpu.VMEM((SCAN_ROWS, 2 * STATE_W), F32),
                        pltpu.VMEM((8, 2 * STATE_W), F32)],
        compiler_params=_params(2),
        name=name,
    )(x3, *consts)


def _s5_weights(lam_re, lam_im, log_dt, b_re, b_im, c_re, c_im):
    dt = jnp.exp(log_dt)[..., None]
    mag = jnp.exp(lam_re * dt)
    abr = mag * jnp.cos(lam_im * dt)
    abi = mag * jnp.sin(lam_im * dt)
    nr, ni = abr - 1.0, abi
    den = lam_re * lam_re + lam_im * lam_im
    cr = (nr * lam_re + ni * lam_im) / den
    ci = (ni * lam_re - nr * lam_im) / den
    bbr = cr[..., None] * b_re - ci[..., None] * b_im
    bbi = cr[..., None] * b_im + ci[..., None] * b_re
    ch_group = jnp.arange(MXU_TILE) // S5_GROUP_CH
    st_group = jnp.arange(STATE_TILE) // S5_STATE

    def blockdiag_in(m):
        m = m.reshape(2, N_CH_TILES, GROUPS_PER_TILE, S5_STATE, S5_GROUP_CH)
        row = m.transpose(0, 1, 4, 2, 3).reshape(2, N_CH_TILES, 1, S5_GROUP_CH, STATE_TILE)
        full = jnp.broadcast_to(row, (2, N_CH_TILES, GROUPS_PER_TILE, S5_GROUP_CH, STATE_TILE))
        full = full.reshape(2, N_CH_TILES, MXU_TILE, STATE_TILE)
        return jnp.where(ch_group[:, None] == st_group[None, :], full, 0.0)

    def blockdiag_out(m):
        m = m.reshape(2, N_CH_TILES, GROUPS_PER_TILE, S5_GROUP_CH, S5_STATE)
        col = m.transpose(0, 1, 2, 4, 3).reshape(2, N_CH_TILES, STATE_TILE, 1, S5_GROUP_CH)
        full = jnp.broadcast_to(col, (2, N_CH_TILES, STATE_TILE, GROUPS_PER_TILE, S5_GROUP_CH))
        full = full.reshape(2, N_CH_TILES, STATE_TILE, MXU_TILE)
        return jnp.where(st_group[:, None] == ch_group[None, :], full, 0.0)

    wb = jnp.stack([blockdiag_in(bbr), blockdiag_in(bbi)], axis=1).astype(BF16)
    wc = jnp.stack([blockdiag_out(c_re), blockdiag_out(-c_im)], axis=1).astype(BF16)
    a = jnp.stack([abr, abi], axis=1).reshape(2, 2, 1, STATE_W)
    pr, pi = abr, abi
    for _ in range(int(math.log2(CHUNK))):
        pr, pi = pr * pr - pi * pi, 2.0 * pr * pi
    a16 = jnp.stack([pr, pi], axis=1).reshape(2, 2, 1, STATE_W)
    return wb, wc, a, a16


def _s5_scan(hr, hm, wb, wc, a, a16):
    x3 = hr.reshape(CHUNK, CHUNK_ROWS, D_MODEL)
    xm3 = jnp.zeros((CHUNK, SCAN_ROWS, D_MODEL), F32).at[:, :BATCH].set(
        hm.reshape(BATCH, N_META, D_MODEL).transpose(1, 0, 2))
    zero_state = jnp.zeros((SCAN_ROWS, 2 * STATE_W), F32)
    fwd = lambda x, init, chain, name: _s5_dir(x, init, wb[0], wc[0], a[0], a16[0], False, chain, name)
    rev = lambda x, init, chain, name: _s5_dir(x, init, wb[1], wc[1], a[1], a16[1], True, chain, name)
    ymf3, meta_end = fwd(xm3, zero_state, False, "s5_fwd_meta")
    yf3, _ = fwd(x3, meta_end, True, "s5_fwd_real")
    yr3, real_end = rev(x3, zero_state, True, "s5_rev_real")
    ymr3, _ = rev(xm3, zero_state.at[:BATCH].set(real_end[:, 0]), False, "s5_rev_meta")
    flat_meta = lambda y3: y3[:, :BATCH].transpose(1, 0, 2).reshape(META_ROWS, D_MODEL)
    return ((yf3.reshape(REAL_ROWS, D_MODEL), yr3.reshape(REAL_ROWS, D_MODEL)),
            (flat_meta(ymf3), flat_meta(ymr3)))


def _rope_tables():
    slot = jnp.arange(CHUNK, dtype=jnp.int32)[:, None, None]
    chunk = jnp.arange(N_CHUNKS, dtype=jnp.int32)[None, None, :]
    tok = jnp.broadcast_to(chunk * CHUNK + slot, (CHUNK, BATCH, N_CHUNKS)).reshape(-1)
    row_id = (tok // GRID_W).astype(F32)
    col_id = (tok % GRID_W).astype(F32)
    inv_freq = ROPE_THETA ** (-jnp.arange(0, ROPE_AXIS_DIM, 2, dtype=F32) / ROPE_AXIS_DIM)
    ang_r = row_id[:, None] * inv_freq[None, :]
    ang_c = col_id[:, None] * inv_freq[None, :]
    cos_r, sin_r, cos_c, sin_c = jnp.cos(ang_r), jnp.sin(ang_r), jnp.cos(ang_c), jnp.sin(ang_c)
    cos_head = jnp.concatenate([cos_r, cos_r, cos_c, cos_c], axis=1)
    sin_head = jnp.concatenate([-sin_r, sin_r, -sin_c, sin_c], axis=1)
    reps = LANES // HEAD_DIM
    cos, sin = jnp.tile(cos_head, (1, reps)), jnp.tile(sin_head, (1, reps))
    cos_meta = jnp.ones((META_ROWS, LANES), F32)
    sin_meta = jnp.zeros((META_ROWS, LANES), F32)
    return cos, sin, cos_meta, sin_meta


def kernel(x, meta_tokens, s5_lambda_re, s5_lambda_im, s5_log_dt, s5_b_re, s5_b_im, s5_c_re, s5_c_im, s5_d,
           s5_w_glu, s5_w_out, attn_w_qkv, attn_q_gain, attn_k_gain, attn_w_out, ffn_w_gate, ffn_w_up,
           ffn_w_down, ln_gain, ln_bias):
    hr = x.reshape(BATCH, N_CHUNKS, CHUNK, D_MODEL).transpose(2, 0, 1, 3).reshape(REAL_ROWS, D_MODEL)
    hm = jnp.broadcast_to(meta_tokens[None], (BATCH, N_META, D_MODEL)).reshape(META_ROWS, D_MODEL)
    cos, sin, cos_meta, sin_meta = _rope_tables()
    head_id = jnp.arange(LANES) // HEAD_DIM
    same_head = ((head_id[:, None] == head_id[None, :]).astype(F32) / HEAD_DIM).astype(BF16)
    ones_blk = jnp.concatenate([same_head, same_head], axis=0)

    for i in range(DEPTH):
        j = i // 2
        ln0 = [ln_gain[i, 0][None], ln_bias[i, 0][None]]
        ln1 = [ln_gain[i, 1][None], ln_bias[i, 1][None]]
        ffn_w = [ffn_w_gate[i].astype(BF16), ffn_w_up[i].astype(BF16), ffn_w_down[i].astype(BF16)]
        if i % 2 == 0:
            wb, wc, a, a16 = _s5_weights(s5_lambda_re[j], s5_lambda_im[j], s5_log_dt[j], s5_b_re[j], s5_b_im[j],
                                         s5_c_re[j], s5_c_im[j])
            (yf, yr), (ymf, ymr) = _s5_scan(hr, hm, wb, wc, a, a16)
            mix_w = [s5_d[j][None], s5_w_glu[j].astype(BF16), s5_w_out[j].astype(BF16)]
            hr, hm = _residual_layer_call(_s5_mix, [hr, yf, yr], [_flat_spec] * 3, [hm, ymf, ymr],
                                          mix_w, ln0, ffn_w, ln1, "s5_tail_ffn")
        else:
            wqkv = attn_w_qkv[j].astype(BF16)
            qg = jnp.tile(attn_q_gain[j], N_Q_HEADS)[None]
            kg = jnp.tile(attn_k_gain[j], N_KV_HEADS)[None]
            q, k, vt = _qkv(hr, wqkv, qg, kg, cos, sin, ones_blk, ROW_TILE, "qkv_real", True)
            qm, km, vm = _qkv(hm, wqkv, qg, kg, cos_meta, sin_meta, ones_blk, N_META, "qkv_meta", False)
            vtm = jnp.concatenate([vm.swapaxes(2, 3), jnp.ones((BATCH, N_KV_HEADS, V_ROWS - HEAD_DIM, N_META), BF16)],
                                  axis=2)
            o = _flash(q, k, vt, km, vtm, Q_TILE, "flash_real")
            om = _flash(qm, k, vt, km, vtm, N_META, "flash_meta").reshape(META_ROWS, D_MODEL)
            hr, hm = _residual_layer_call(_attn_mix, [hr, o], [_flat_spec, _batch_major_spec], [hm, om],
                                          [attn_w_out[j].astype(BF16)], ln0, ffn_w, ln1, "attn_tail_ffn")
    return hr.reshape(CHUNK, BATCH, N_CHUNKS, D_MODEL).transpose(1, 2, 0, 3).reshape(BATCH, SEQ, D_MODEL)
```
